```python
import math
import jax
import jax.numpy as jnp
from jax import lax
import numpy as np


D_MODEL = 4096
BATCH = 1
SEQ = 16384
DEPTH = 4

HEAD_DIM = 128
W_A = D_MODEL // 4
W_B = D_MODEL // 2
W_C = D_MODEL - W_A - W_B
D_MIX = W_A + W_B + W_C
H_A = W_A // HEAD_DIM
DQK_A = HEAD_DIM // 2
DV_A = HEAD_DIM
GATE_CAP = 15.0
H_B = W_B // HEAD_DIM
HKV_B = 4
Q_LORA = 896
IDX_H = 16
IDX_D = 64
TOPK = 256
QBLK = 128
H_C = W_C // HEAD_DIM
DK_C = HEAD_DIM
DV_C = HEAD_DIM
CONV_K = 4
CHUNK = 64
EPS = 1e-6

IN_SPLITS = (H_A * DQK_A, H_A * DQK_A, W_A, H_A, H_A, W_A, W_A,
             Q_LORA, HKV_B * HEAD_DIM, HKV_B * HEAD_DIM, IDX_D, IDX_H, W_B,
             3 * W_C, H_C, H_C, W_C)
N_IN = sum(IN_SPLITS)

kernel_name = "hymba_style_mlstm_dsa_gdn_hybrid"


def rms_norm(x, g):
    xf = x.astype(jnp.float32)
    y = xf * lax.rsqrt(jnp.mean(xf * xf, axis=-1, keepdims=True) + EPS)
    return (y * g.astype(jnp.float32)).astype(x.dtype)


def layer_norm(x, g, b):
    xf = x.astype(jnp.float32)
    mu = jnp.mean(xf, axis=-1, keepdims=True)
    xc = xf - mu
    y = xc * lax.rsqrt(jnp.mean(xc * xc, axis=-1, keepdims=True) + EPS)
    return (y * g.astype(jnp.float32) + b.astype(jnp.float32)).astype(x.dtype)


def l2_normalize(x):
    xf = x.astype(jnp.float32)
    return (xf * lax.rsqrt(jnp.sum(xf * xf, axis=-1, keepdims=True) + EPS)).astype(x.dtype)


def soft_cap(x):
    return GATE_CAP * jnp.tanh(x / GATE_CAP)


def causal_depthwise_conv(x, w):
    width, ch = w.shape
    return lax.conv_general_dilated(x, w[:, None, :].astype(x.dtype), window_strides=(1,),
                                    padding=[(width - 1, 0)],
                                    dimension_numbers=('NWC', 'WIO', 'NWC'),
                                    feature_group_count=ch)


def to_chunks(a, n_chunks):
    b = a.shape[0]
    a = a.astype(jnp.float32).reshape((b, n_chunks, CHUNK) + a.shape[2:])
    return jnp.moveaxis(jnp.moveaxis(a, 1, 0), 2, 3)


def from_chunks(a):
    n, b, h, l, d = a.shape
    return jnp.moveaxis(jnp.moveaxis(a, 3, 2), 0, 1).reshape(b, n * l, h, d)


def mlstm_chunked(q, k, v, i_pre, f_pre):
    b, s, h, dk = q.shape
    dv = v.shape[-1]
    n_chunks = s // CHUNK
    qc = to_chunks(q, n_chunks) * dk ** -0.5
    kc = to_chunks(k, n_chunks)
    vc = to_chunks(v, n_chunks)
    log_f = jax.nn.log_sigmoid(to_chunks(f_pre, n_chunks))
    log_i = to_chunks(i_pre, n_chunks)
    causal = jnp.tril(jnp.ones((CHUNK, CHUNK), bool))

    def step(carry, xs):
        c_st, n_st, m_st = carry
        qj, kj, vj, lfj, lij = xs
        bj = jnp.cumsum(lfj, axis=-1)
        d_log = jnp.where(causal, bj[..., :, None] - bj[..., None, :] + lij[..., None, :], -jnp.inf)
        inter = bj + m_st[..., None]
        m_j = jnp.maximum(inter, jnp.max(d_log, axis=-1))
        p = jnp.exp(d_log - m_j[..., None]) * jnp.einsum('bhjd,bhsd->bhjs', qj, kj)
        w_inter = jnp.exp(inter - m_j)
        num = p @ vj + w_inter[..., None] * jnp.einsum('bhjd,bhdv->bhjv', qj, c_st)
        den = jnp.sum(p, axis=-1) + w_inter * jnp.einsum('bhjd,bhd->bhj', qj, n_st)
        h_out = num / jnp.maximum(jnp.abs(den), jnp.exp(-m_j))[..., None]
        u = bj[..., -1:] - bj + lij
        m_new = jnp.maximum(bj[..., -1] + m_st, jnp.max(u, axis=-1))
        w_u = jnp.exp(u - m_new[..., None])
        decay = jnp.exp(bj[..., -1] + m_st - m_new)
        c_st = decay[..., None, None] * c_st + jnp.einsum('bhs,bhsd,bhsv->bhdv', w_u, kj, vj)
        n_st = decay[..., None] * n_st + jnp.einsum('bhs,bhsd->bhd', w_u, kj)
        return (c_st, n_st, m_new), h_out

    init = (jnp.zeros((b, h, dk, dv), jnp.float32), jnp.zeros((b, h, dk), jnp.float32),
            jnp.zeros((b, h), jnp.float32))
    _, hs = lax.scan(step, init, (qc, kc, vc, log_f, log_i))
    return from_chunks(hs).astype(v.dtype)


def gated_delta_chunked(q, k, v, g, beta):
    b, s, h, dk = q.shape
    dv = v.shape[-1]
    n_chunks = s // CHUNK
    qc = to_chunks(q, n_chunks) * dk ** -0.5
    kc = to_chunks(k, n_chunks)
    vc = to_chunks(v, n_chunks)
    gc = jnp.cumsum(to_chunks(g, n_chunks), axis=-1)
    bc = to_chunks(beta, n_chunks)
    causal = jnp.tril(jnp.ones((CHUNK, CHUNK), bool))
    strict = jnp.tril(jnp.ones((CHUNK, CHUNK), bool), -1)
    diff = gc[..., :, None] - gc[..., None, :]
    gamma = jnp.where(causal, jnp.exp(jnp.where(causal, diff, 0.0)), 0.0)
    kb = kc * bc[..., None]
    a_kk = jnp.where(strict, jnp.einsum('nbhid,nbhjd->nbhij', kb, kc) * gamma, 0.0)
    eye = jnp.eye(CHUNK, dtype=jnp.float32)
    rhs = jnp.concatenate([vc * bc[..., None], kb * jnp.exp(gc)[..., None]], axis=-1)
    sol = lax.linalg.triangular_solve(eye + a_kk, rhs, left_side=True, lower=True)
    u, w = sol[..., :dv], sol[..., dv:]
    a_qk = jnp.einsum('nbhid,nbhjd->nbhij', qc, kc) * gamma
    decay_q = jnp.exp(gc)
    decay_k = jnp.exp(gc[..., -1:] - gc)

    def step(state, xs):
        qj, kj, uj, wj, aj, dqj, dkj, glast = xs
        v_new = uj - wj @ state
        o = (qj * dqj[..., None]) @ state + aj @ v_new
        state = jnp.exp(glast)[..., None, None] * state + jnp.einsum('bhld,bhlv->bhdv', kj * dkj[..., None], v_new)
        return state, o

    s0 = jnp.zeros((b, h, dk, dv), jnp.float32)
    _, o = lax.scan(step, s0, (qc, kc, u, w, a_qk, decay_q, decay_k, gc[..., -1]))
    return from_chunks(o).astype(v.dtype)


def dsa_sparse_attention(q, k, v, q_idx, k_idx, w_idx, k_sel):
    b, s, h, dh = q.shape
    hkv = k.shape[2]
    nb = s // QBLK

    def blocks(a):
        return jnp.moveaxis(a.reshape((b, nb, QBLK) + a.shape[2:]), 1, 0)

    q_blocks = blocks(q.reshape(b, s, hkv, h // hkv, dh))
    qi_blocks = blocks(q_idx)
    w_blocks = blocks(w_idx)
    pos_blocks = jnp.arange(s).reshape(nb, QBLK)
    key_pos = jnp.arange(s)
    gather = jax.vmap(lambda seq, ind: seq[ind])

    def one_block(args):
        q_blk, qi_blk, w_blk, t = args
        logits = jnp.einsum('bqhd,bsd->bqhs', qi_blk, k_idx)
        score = jnp.einsum('bqhs,bqh->bqs', jax.nn.relu(logits), w_blk).astype(jnp.float32)
        score = jnp.where((key_pos[None, :] <= t[:, None])[None], score, -jnp.inf)
        _, idx = lax.top_k(score, k_sel)
        valid = idx <= t[None, :, None]
        k_g = gather(k, idx)
        v_g = gather(v, idx)
        sc = jnp.einsum('bqhgd,bqkhd->bqhgk', q_blk, k_g).astype(jnp.float32) * dh ** -0.5
        sc = jnp.where(valid[:, :, None, None, :], sc, -jnp.inf)
        p = jax.nn.softmax(sc, axis=-1).astype(v.dtype)
        return jnp.einsum('bqhgk,bqkhd->bqhgd', p, v_g)

    out = lax.map(one_block, (q_blocks, qi_blocks, w_blocks, pos_blocks))
    return jnp.moveaxis(out, 0, 1).reshape(b, s, h * dh)


def setup_inputs(seed: int = 0) -> dict:
    key = jax.random.key(seed)
    ks = jax.random.split(key, 18)
    f32 = jnp.float32
    L = DEPTH

    def nrm(k, shape, scale):
        return scale * jax.random.normal(k, shape, f32)

    x = jax.random.normal(ks[0], (BATCH, SEQ, D_MODEL), f32)
    norm_g = 1.0 + nrm(ks[1], (L, D_MODEL), 0.02)
    w_in = nrm(ks[2], (L, D_MODEL, N_IN), D_MODEL ** -0.5)
    w_out = nrm(ks[3], (L, D_MIX, D_MODEL), D_MIX ** -0.5)
    m_ib = nrm(ks[4], (L, H_A), 0.1)
    m_fb = 3.0 + nrm(ks[5], (L, H_A), 0.5)
    m_norm_g = 1.0 + nrm(ks[6], (L, W_A), 0.02)
    a_cq_g = 1.0 + nrm(ks[7], (L, Q_LORA), 0.02)
    a_wuq = nrm(ks[8], (L, Q_LORA, W_B), Q_LORA ** -0.5)
    a_wuqi = nrm(ks[9], (L, Q_LORA, IDX_H * IDX_D), Q_LORA ** -0.5)
    a_qn_g = 1.0 + nrm(ks[10], (L, HEAD_DIM), 0.02)
    a_kn_g = 1.0 + nrm(ks[11], (L, HEAD_DIM), 0.02)
    a_ki_g = 1.0 + nrm(ks[12], (L, IDX_D), 0.02)
    a_ki_b = nrm(ks[13], (L, IDX_D), 0.02)
    d_conv_w = nrm(ks[14], (L, CONV_K, 3 * W_C), CONV_K ** -0.5)
    d_a_log = jnp.log(jax.random.uniform(ks[15], (L, H_C), f32, 1.0, 16.0))
    dt = jnp.exp(jax.random.uniform(ks[16], (L, H_C), f32, math.log(1e-3), math.log(1e-1)))
    d_dt_bias = dt + jnp.log(-jnp.expm1(-dt))
    d_norm_g = 1.0 + nrm(ks[17], (L, DV_C), 0.02)
    return {"x": x, "norm_g": norm_g, "w_in": w_in, "w_out": w_out,
            "m_ib": m_ib, "m_fb": m_fb, "m_norm_g": m_norm_g,
            "a_cq_g": a_cq_g, "a_wuq": a_wuq, "a_wuqi": a_wuqi, "a_qn_g": a_qn_g,
            "a_kn_g": a_kn_g, "a_ki_g": a_ki_g, "a_ki_b": a_ki_b,
            "d_conv_w": d_conv_w, "d_a_log": d_a_log, "d_dt_bias": d_dt_bias, "d_norm_g": d_norm_g}


def reference(x, norm_g, w_in, w_out, m_ib, m_fb, m_norm_g, a_cq_g, a_wuq, a_wuqi, a_qn_g,
              a_kn_g, a_ki_g, a_ki_b, d_conv_w, d_a_log, d_dt_bias, d_norm_g):
    B, S, _ = x.shape
    k_sel = min(TOPK, S // 4)
    split_points = np.cumsum(IN_SPLITS)[:-1].tolist()
    for l in range(DEPTH):
        h = rms_norm(x, norm_g[l])
        proj = jnp.einsum('bsd,dn->bsn', h, w_in[l])
        (aq, ak, av, ai, af, ao, az,
         bcq, bk, bv, bki, bwi, bz,
         cqkv, cb, ca, cz) = jnp.split(proj, split_points, axis=-1)

        h_a = mlstm_chunked(aq.reshape(B, S, H_A, DQK_A), ak.reshape(B, S, H_A, DQK_A),
                            av.reshape(B, S, H_A, DV_A),
                            soft_cap(ai + m_ib[l]), soft_cap(af + m_fb[l]))
        h_a = rms_norm(h_a, m_norm_g[l].reshape(H_A, DV_A)).reshape(B, S, W_A)
        y_a = jax.nn.sigmoid(ao) * h_a * jax.nn.silu(az)

        c_q = rms_norm(bcq, a_cq_g[l])
        q_b = rms_norm(jnp.einsum('bsr,rn->bsn', c_q, a_wuq[l]).reshape(B, S, H_B, HEAD_DIM), a_qn_g[l])
        k_b = rms_norm(bk.reshape(B, S, HKV_B, HEAD_DIM), a_kn_g[l])
        v_b = bv.reshape(B, S, HKV_B, HEAD_DIM)
        q_idx = jnp.einsum('bsr,rn->bsn', c_q, a_wuqi[l]).reshape(B, S, IDX_H, IDX_D)
        k_idx = layer_norm(bki, a_ki_g[l], a_ki_b[l])
        w_idx = bwi * (IDX_H ** -0.5 * IDX_D ** -0.5)
        y_b = dsa_sparse_attention(q_b, k_b, v_b, q_idx, k_idx, w_idx, k_sel) * jax.nn.silu(bz)

        qkv = jax.nn.silu(causal_depthwise_conv(cqkv, d_conv_w[l]))
        q_c, k_c, v_c = jnp.split(qkv, 3, axis=-1)
        q_c = l2_normalize(q_c.reshape(B, S, H_C, DK_C))
        k_c = l2_normalize(k_c.reshape(B, S, H_C, DK_C))
        beta = jax.nn.sigmoid(cb.astype(jnp.float32))
        g = -jnp.exp(d_a_log[l].astype(jnp.float32)) * jax.nn.softplus(
            ca.astype(jnp.float32) + d_dt_bias[l].astype(jnp.float32))
        o_c = gated_delta_chunked(q_c, k_c, v_c.reshape(B, S, H_C, DV_C), g, beta)
        y_c = (rms_norm(o_c, d_norm_g[l]) * jax.nn.silu(cz.reshape(B, S, H_C, DV_C))).reshape(B, S, W_C)

        y = jnp.concatenate([y_a, y_b, y_c], axis=-1)
        x = x + jnp.einsum('bsm,md->bsd', y, w_out[l])
    return x
```

```python
import functools

import jax
import jax.numpy as jnp
import numpy as np
from jax import lax
from jax.experimental import pallas as pl
from jax.experimental.pallas import tpu as pltpu

F32 = jnp.float32
BF16 = jnp.bfloat16
HI = lax.Precision.HIGHEST

HEAD_DIM = 128
H_A = 8
DQK_A = 64
GATE_CAP = 15.0
H_B = 16
HKV_B = 4
Q_LORA = 896
IDX_H = 16
IDX_D = 64
TOPK = 256
QBLK = 128
H_C = 8
CONV_K = 4
EPS = 1e-6

OFF_BCQ = 0
OFF_SMALL = 896
OFF_AQ = 1024
OFF_AK = 1536
OFF_AV = 2048
OFF_AO = 3072
OFF_AZ = 4096
OFF_CQ = 5120
OFF_CK = 6144
OFF_CV = 7168
OFF_CZ = 8192
OFF_BK = 9216
OFF_BV = 9728
OFF_BZ = 10240
N_PROJ = 12288
SM_KI = 0
SM_AI = 64
SM_AF = 72
SM_WI = 80
SM_CB = 96
SM_CA = 104

NEG_BIG = -1e30
INT_MIN = -2147483648
KEY_NEG_INF = -2139095041

VMEM_LIMIT = 56 * 1024 * 1024


def _dot(a, b, prec=None):
    return jnp.dot(a, b, preferred_element_type=F32, precision=prec)


def _dot_nt(a, b, prec=None):
    return lax.dot_general(a, b, (((1,), (1,)), ((), ())), preferred_element_type=F32,
                           precision=prec)


def _dot_tn(a, b, prec=None):
    return lax.dot_general(a, b, (((0,), (0,)), ((), ())), preferred_element_type=F32,
                           precision=prec)


def _iota2(shape, dim):
    return lax.broadcasted_iota(jnp.int32, shape, dim)


def _log_sigmoid(x):
    return jnp.minimum(x, 0.0) - jnp.log(1.0 + jnp.exp(-jnp.abs(x)))


def _softplus(x):
    return jnp.maximum(x, 0.0) + jnp.log(1.0 + jnp.exp(-jnp.abs(x)))


def _soft_cap(x):
    return GATE_CAP * jnp.tanh(x / GATE_CAP)


def _silu(x):
    return x * jax.nn.sigmoid(x)


def _params(sem):
    return pltpu.CompilerParams(dimension_semantics=sem, vmem_limit_bytes=VMEM_LIMIT)


def _rmsnorm_kernel(x_ref, g_ref, o_ref):
    x = x_ref[...]
    ms = jnp.mean(x * x, axis=-1, keepdims=True)
    o_ref[...] = (x * lax.rsqrt(ms + EPS) * g_ref[...]).astype(o_ref.dtype)


def _rmsnorm(x, g, tm=512):
    s, d = x.shape
    return pl.pallas_call(
        _rmsnorm_kernel,
        grid=(s // tm,),
        in_specs=[pl.BlockSpec((tm, d), lambda i: (i, 0)),
                  pl.BlockSpec((1, d), lambda i: (0, 0))],
        out_specs=pl.BlockSpec((tm, d), lambda i: (i, 0)),
        out_shape=jax.ShapeDtypeStruct((s, d), BF16),
        compiler_params=_params(("parallel",)),
        name="rmsnorm",
    )(x, g.reshape(1, d))


def _matmul_kernel(a_ref, b_ref, o_ref):
    o_ref[...] = _dot(a_ref[...], b_ref[...])


def _in_proj(h, w, tm=1024, tn=512):
    s, d = h.shape
    n = w.shape[1]
    tm = min(tm, s)
    return pl.pallas_call(
        _matmul_kernel,
        grid=(s // tm, n // tn),
        in_specs=[pl.BlockSpec((tm, d), lambda i, j: (i, 0)),
                  pl.BlockSpec((d, tn), lambda i, j: (0, j))],
        out_specs=pl.BlockSpec((tm, tn), lambda i, j: (i, j)),
        out_shape=jax.ShapeDtypeStruct((s, n), F32),
        compiler_params=_params(("parallel", "arbitrary")),
        name="in_proj",
    )(h, w)


def _out_proj_kernel(x_ref, ya_ref, yb_ref, yc_ref, wa_ref, wb_ref, wc_ref, o_ref):
    acc = _dot(ya_ref[...], wa_ref[...])
    acc = acc + _dot(yb_ref[...], wb_ref[...])
    acc = acc + _dot(yc_ref[...], wc_ref[...])
    o_ref[...] = x_ref[...] + acc


def _out_proj(x, ya, yb, yc, wa, wb, wc, tm=1024, tn=512):
    s, d = x.shape
    tm = min(tm, s)
    na, nb, nc = ya.shape[1], yb.shape[1], yc.shape[1]
    return pl.pallas_call(
        _out_proj_kernel,
        grid=(s // tm, d // tn),
        in_specs=[pl.BlockSpec((tm, tn), lambda i, j: (i, j)),
                  pl.BlockSpec((tm, na), lambda i, j: (i, 0)),
                  pl.BlockSpec((tm, nb), lambda i, j: (i, 0)),
                  pl.BlockSpec((tm, nc), lambda i, j: (i, 0)),
                  pl.BlockSpec((na, tn), lambda i, j: (0, j)),
                  pl.BlockSpec((nb, tn), lambda i, j: (0, j)),
                  pl.BlockSpec((nc, tn), lambda i, j: (0, j))],
        out_specs=pl.BlockSpec((tm, tn), lambda i, j: (i, j)),
        out_shape=jax.ShapeDtypeStruct((s, d), F32),
        compiler_params=_params(("parallel", "arbitrary")),
        name="out_proj",
    )(x, ya, yb, yc, wa, wb, wc)


def _mlstm_kernel(q_ref, k_ref, v_ref, o_ref, z_ref, sm_ref, bias_ref, ng_ref, y_ref,
                  c_ref, n_ref, m_ref, *, L):
    @pl.when(pl.program_id(0) == 0)
    def _():
        c_ref[...] = jnp.zeros_like(c_ref)
        n_ref[...] = jnp.zeros_like(n_ref)
        m_ref[...] = jnp.zeros_like(m_ref)

    sm = sm_ref[...] + bias_ref[...]
    gcol = _soft_cap(sm)
    lscol = _log_sigmoid(gcol)
    grow = _soft_cap(sm.T[SM_AI:SM_AI + 16, :])
    lsrow = _log_sigmoid(grow)
    ri = _iota2((L, L), 0)
    ci = _iota2((L, L), 1)
    causal = ci <= ri
    tril = jnp.where(causal, 1.0, 0.0).astype(F32)
    triu = jnp.where(ri <= ci, 1.0, 0.0).astype(F32)
    bcol = _dot(tril, lscol, HI)
    brow = _dot(lsrow, triu, HI)
    scale = DQK_A ** -0.5

    for h in range(H_A):
        b_c = bcol[:, SM_AF + h:SM_AF + h + 1]
        li_c = gcol[:, SM_AI + h:SM_AI + h + 1]
        b_r = brow[8 + h:9 + h, :]
        li_r = grow[h:h + 1, :]
        b_last = b_r[:, L - 1:L]
        m_st = m_ref[h:h + 1, 0:1]
        qh = (q_ref[:, h * DQK_A:(h + 1) * DQK_A] * scale).astype(BF16)
        kf = k_ref[:, h * DQK_A:(h + 1) * DQK_A]
        kh = kf.astype(BF16)
        vh = v_ref[:, h * HEAD_DIM:(h + 1) * HEAD_DIM].astype(BF16)
        c_st = c_ref[h]
        n_st = n_ref[h:h + 1, :]

        d_log = jnp.where(causal, b_c - b_r + li_r, -jnp.inf)
        inter = b_c + m_st
        m_j = jnp.maximum(inter, jnp.max(d_log, axis=-1, keepdims=True))
        p = jnp.exp(d_log - m_j) * _dot_nt(qh, kh)
        w_inter = jnp.exp(inter - m_j)
        num = _dot(p.astype(BF16), vh) + w_inter * _dot(qh, c_st.astype(BF16))
        qn = jnp.sum(qh.astype(F32) * n_st, axis=-1, keepdims=True)
        den = jnp.sum(p, axis=-1, keepdims=True) + w_inter * qn
        h_out = num / jnp.maximum(jnp.abs(den), jnp.exp(-m_j))

        u_c = b_last - b_c + li_c
        m_new = jnp.maximum(b_last + m_st, jnp.max(u_c, axis=0, keepdims=True))
        w_u = jnp.exp(u_c - m_new)
        decay = jnp.exp(b_last + m_st - m_new)
        kw = kf * w_u
        c_ref[h] = decay * c_st + _dot_tn(kw.astype(BF16), vh)
        n_ref[h:h + 1, :] = decay * n_st + jnp.sum(kw, axis=0, keepdims=True)
        m_ref[h:h + 1, :] = jnp.broadcast_to(m_new, (1, 128))

        g = ng_ref[:, h * HEAD_DIM:(h + 1) * HEAD_DIM]
        hn = h_out * lax.rsqrt(jnp.mean(h_out * h_out, axis=-1, keepdims=True) + EPS) * g
        oh = o_ref[:, h * HEAD_DIM:(h + 1) * HEAD_DIM]
        zh = z_ref[:, h * HEAD_DIM:(h + 1) * HEAD_DIM]
        y_ref[:, h * HEAD_DIM:(h + 1) * HEAD_DIM] = (jax.nn.sigmoid(oh) * hn * _silu(zh)).astype(BF16)


def _mlstm(proj, bias_row, norm_g, L=256):
    s = proj.shape[0]
    L = min(L, s)
    return pl.pallas_call(
        functools.partial(_mlstm_kernel, L=L),
        grid=(s // L,),
        in_specs=[pl.BlockSpec((L, 512), lambda i: (i, OFF_AQ // 512)),
                  pl.BlockSpec((L, 512), lambda i: (i, OFF_AK // 512)),
                  pl.BlockSpec((L, 1024), lambda i: (i, OFF_AV // 1024)),
                  pl.BlockSpec((L, 1024), lambda i: (i, OFF_AO // 1024)),
                  pl.BlockSpec((L, 1024), lambda i: (i, OFF_AZ // 1024)),
                  pl.BlockSpec((L, 128), lambda i: (i, OFF_SMALL // 128)),
                  pl.BlockSpec((1, 128), lambda i: (0, 0)),
                  pl.BlockSpec((1, 1024), lambda i: (0, 0))],
        out_specs=pl.BlockSpec((L, 1024), lambda i: (i, 0)),
        out_shape=jax.ShapeDtypeStruct((s, 1024), BF16),
        scratch_shapes=[pltpu.VMEM((H_A, DQK_A, HEAD_DIM), F32),
                        pltpu.VMEM((H_A, DQK_A), F32),
                        pltpu.VMEM((H_A, 128), F32)],
        compiler_params=_params(("arbitrary",)),
        name="mlstm",
    )(proj, proj, proj, proj, proj, proj, bias_row, norm_g)


GC = 64


def _gdn_kernel(q_ref, k_ref, v_ref, hq_ref, hk_ref, hv_ref, z_ref, sm_ref, cw_ref,
                arow_ref, acol_ref, dtrow_ref, dtcol_ref, ng_ref, y_ref, s_ref):
    i = pl.program_id(0)

    @pl.when(i == 0)
    def _():
        s_ref[...] = jnp.zeros_like(s_ref)

    def conv_silu(x_ref, h_ref, w):
        halo = jnp.where(i > 0, h_ref[...], 0.0)
        xf = jnp.concatenate([halo, x_ref[...]], axis=0)
        y = w[3:4, :] * xf[8:8 + GC]
        for sh in range(1, CONV_K):
            y = y + w[3 - sh:4 - sh, :] * pltpu.roll(xf, sh, 0)[8:8 + GC]
        return _silu(y)

    qc = conv_silu(q_ref, hq_ref, cw_ref[:, 0:1024])
    kc = conv_silu(k_ref, hk_ref, cw_ref[:, 1024:2048])
    vc = conv_silu(v_ref, hv_ref, cw_ref[:, 2048:3072])

    sm = sm_ref[...]
    smT = sm.T
    beta_col = jax.nn.sigmoid(sm)
    g_col = -jnp.exp(arow_ref[...]) * _softplus(sm + dtrow_ref[...])
    g_row = -jnp.exp(acol_ref[SM_CA:SM_CA + 8, :]) * _softplus(
        smT[SM_CA:SM_CA + 8, :] + dtcol_ref[SM_CA:SM_CA + 8, :])
    ri = _iota2((GC, GC), 0)
    ci = _iota2((GC, GC), 1)
    causal = ci <= ri
    strict = ci < ri
    bdiag = (ri // 16) == (ci // 16)
    eye = jnp.where(ri == ci, 1.0, 0.0).astype(F32)
    tril = jnp.where(causal, 1.0, 0.0).astype(F32)
    triu = jnp.where(ri <= ci, 1.0, 0.0).astype(F32)
    gc_col = _dot(tril, g_col, HI)
    gc_row = _dot(g_row, triu, HI)
    scale = HEAD_DIM ** -0.5

    for h in range(H_C):
        sl = slice(h * HEAD_DIM, (h + 1) * HEAD_DIM)
        qh = qc[:, sl]
        kh = kc[:, sl]
        vh = vc[:, sl]
        qh = qh * lax.rsqrt(jnp.sum(qh * qh, axis=-1, keepdims=True) + EPS) * scale
        kh = kh * lax.rsqrt(jnp.sum(kh * kh, axis=-1, keepdims=True) + EPS)
        gc_c = gc_col[:, SM_CA + h:SM_CA + h + 1]
        gc_r = gc_row[h:h + 1, :]
        beta_c = beta_col[:, SM_CB + h:SM_CB + h + 1]
        g_last = gc_c[GC - 1:GC, :]
        gamma = jnp.where(causal, jnp.exp(jnp.where(causal, gc_c - gc_r, 0.0)), 0.0)
        kb = kh * beta_c
        kh_b = kh.astype(BF16)
        a = jnp.where(strict, _dot_nt(kb.astype(BF16), kh_b) * gamma, 0.0)
        ad = jnp.where(bdiag, a, 0.0)
        ao = a - ad
        a2 = _dot(ad, ad, HI)
        a4 = _dot(a2, a2, HI)
        a8 = _dot(a4, a4, HI)
        dinv = eye - ad
        dinv = dinv + _dot(dinv, a2, HI)
        dinv = dinv + _dot(dinv, a4, HI)
        dinv = dinv + _dot(dinv, a8, HI)
        nn = _dot(dinv, ao, HI)
        n2 = _dot(nn, nn, HI)
        ninv = eye - nn
        ninv = ninv + _dot(ninv, n2, HI)
        tinv = _dot(ninv, dinv, HI)
        u = _dot(tinv, vh * beta_c, HI)
        w = _dot(tinv, kb * jnp.exp(gc_c), HI)
        a_qk = _dot_nt(qh.astype(BF16), kh_b) * gamma

        st = s_ref[h]
        st_b = st.astype(BF16)
        v_new = u - _dot(w.astype(BF16), st_b)
        o = _dot((qh * jnp.exp(gc_c)).astype(BF16), st_b) + _dot(a_qk.astype(BF16),
                                                                  v_new.astype(BF16))
        kd = kh * jnp.exp(g_last - gc_c)
        s_ref[h] = jnp.exp(g_last) * st + _dot_tn(kd.astype(BF16), v_new.astype(BF16))

        on = o * lax.rsqrt(jnp.mean(o * o, axis=-1, keepdims=True) + EPS) * ng_ref[...]
        y_ref[:, sl] = (on * _silu(z_ref[:, sl])).astype(BF16)


def _gdn(proj, conv_w, arow, acol, dtrow, dtcol, norm_g):
    s = proj.shape[0]

    def halo(col):
        return pl.BlockSpec((8, 1024), lambda i: (jnp.maximum(i * (GC // 8) - 1, 0), col))

    return pl.pallas_call(
        _gdn_kernel,
        grid=(s // GC,),
        in_specs=[pl.BlockSpec((GC, 1024), lambda i: (i, OFF_CQ // 1024)),
                  pl.BlockSpec((GC, 1024), lambda i: (i, OFF_CK // 1024)),
                  pl.BlockSpec((GC, 1024), lambda i: (i, OFF_CV // 1024)),
                  halo(OFF_CQ // 1024), halo(OFF_CK // 1024), halo(OFF_CV // 1024),
                  pl.BlockSpec((GC, 1024), lambda i: (i, OFF_CZ // 1024)),
                  pl.BlockSpec((GC, 128), lambda i: (i, OFF_SMALL // 128)),
                  pl.BlockSpec((CONV_K, 3072), lambda i: (0, 0)),
                  pl.BlockSpec((1, 128), lambda i: (0, 0)),
                  pl.BlockSpec((128, 1), lambda i: (0, 0)),
                  pl.BlockSpec((1, 128), lambda i: (0, 0)),
                  pl.BlockSpec((128, 1), lambda i: (0, 0)),
                  pl.BlockSpec((1, 128), lambda i: (0, 0))],
        out_specs=pl.BlockSpec((GC, 1024), lambda i: (i, 0)),
        out_shape=jax.ShapeDtypeStruct((s, 1024), BF16),
        scratch_shapes=[pltpu.VMEM((H_C, HEAD_DIM, HEAD_DIM), F32)],
        compiler_params=_params(("arbitrary",)),
        name="gdn",
    )(proj, proj, proj, proj, proj, proj, proj, proj, conv_w, arow, acol, dtrow, dtcol, norm_g)


def _dsa_prep_kernel(cq_ref, k_ref, v_ref, sm_ref, cqg_ref, wuq_ref, wuqi_ref, qng_ref,
                     kng_ref, kig_ref, kib_ref,
                     qb_ref, qi_ref, w_ref, kt_ref, vb_ref, kit_ref):
    cq = cq_ref[...]
    cq = cq * lax.rsqrt(jnp.mean(cq * cq, axis=-1, keepdims=True) + EPS) * cqg_ref[...]
    cq = cq.astype(BF16)
    qf = _dot(cq, wuq_ref[...])
    qscale = HEAD_DIM ** -0.5
    for h in range(H_B):
        qh = qf[:, h * HEAD_DIM:(h + 1) * HEAD_DIM]
        qh = qh * lax.rsqrt(jnp.mean(qh * qh, axis=-1, keepdims=True) + EPS) * qng_ref[...]
        qb_ref[h] = (qh * qscale).astype(BF16)
        qi_ref[h] = _dot(cq, wuqi_ref[h]).astype(BF16)
    kn = []
    for g in range(HKV_B):
        kg = k_ref[:, g * HEAD_DIM:(g + 1) * HEAD_DIM]
        kn.append(kg * lax.rsqrt(jnp.mean(kg * kg, axis=-1, keepdims=True) + EPS) * kng_ref[...])
    kt_ref[...] = jnp.concatenate(kn, axis=1).T.astype(BF16)
    vb_ref[...] = v_ref[...].astype(BF16)
    sm = sm_ref[...]
    w_ref[...] = sm * (IDX_H ** -0.5 * IDX_D ** -0.5)
    lane = _iota2(sm.shape, 1)
    is_ki = lane < IDX_D
    mu = jnp.sum(jnp.where(is_ki, sm, 0.0), axis=-1, keepdims=True) * (1.0 / IDX_D)
    xc = jnp.where(is_ki, sm - mu, 0.0)
    var = jnp.sum(xc * xc, axis=-1, keepdims=True) * (1.0 / IDX_D)
    ki = xc * lax.rsqrt(var + EPS) * kig_ref[...] + kib_ref[...]
    kit_ref[...] = ki.T[0:IDX_D, :].astype(BF16)


def _dsa_prep(proj, cq_g, wuq, wuqi_r, qn_g, kn_g, ki_g, ki_b, tt=256):
    s = proj.shape[0]
    tt = min(tt, s)
    const2 = lambda i: (0, 0)
    return pl.pallas_call(
        _dsa_prep_kernel,
        grid=(s // tt,),
        in_specs=[pl.BlockSpec((tt, Q_LORA), lambda i: (i, 0)),
                  pl.BlockSpec((tt, 512), lambda i: (i, OFF_BK // 512)),
                  pl.BlockSpec((tt, 512), lambda i: (i, OFF_BV // 512)),
                  pl.BlockSpec((tt, 128), lambda i: (i, OFF_SMALL // 128)),
                  pl.BlockSpec((1, Q_LORA), const2),
                  pl.BlockSpec((Q_LORA, H_B * HEAD_DIM), const2),
                  pl.BlockSpec((IDX_H, Q_LORA, IDX_D), lambda i: (0, 0, 0)),
                  pl.BlockSpec((1, HEAD_DIM), const2),
                  pl.BlockSpec((1, HEAD_DIM), const2),
                  pl.BlockSpec((1, 128), const2),
                  pl.BlockSpec((1, 128), const2)],
        out_specs=[pl.BlockSpec((H_B, tt, HEAD_DIM), lambda i: (0, i, 0)),
                   pl.BlockSpec((IDX_H, tt, IDX_D), lambda i: (0, i, 0)),
                   pl.BlockSpec((tt, 128), lambda i: (i, 0)),
                   pl.BlockSpec((HKV_B * HEAD_DIM, tt), lambda i: (0, i)),
                   pl.BlockSpec((tt, HKV_B * HEAD_DIM), lambda i: (i, 0)),
                   pl.BlockSpec((IDX_D, tt), lambda i: (0, i))],
        out_shape=[jax.ShapeDtypeStruct((H_B, s, HEAD_DIM), BF16),
                   jax.ShapeDtypeStruct((IDX_H, s, IDX_D), BF16),
                   jax.ShapeDtypeStruct((s, 128), F32),
                   jax.ShapeDtypeStruct((HKV_B * HEAD_DIM, s), BF16),
                   jax.ShapeDtypeStruct((s, HKV_B * HEAD_DIM), BF16),
                   jax.ShapeDtypeStruct((IDX_D, s), BF16)],
        compiler_params=_params(("parallel",)),
        name="dsa_prep",
    )(proj, proj, proj, proj, cq_g, wuq, wuqi_r, qn_g, kn_g, ki_g, ki_b)


def _dsa_kernel(qb_ref, qi_ref, w_ref, kit_ref, kt_ref, vb_ref, z_ref, y_ref,
                keys_ref, wb_ref, thr_ref, pbuf_ref, m_ref, l_ref, acc_ref, *, TK, KSEL):
    i = pl.program_id(0)
    g = pl.program_id(1)
    n_t = ((i + 1) * QBLK + TK - 1) // TK
    NC = TK // 128
    GH = H_B // HKV_B

    @pl.when(g == 0)
    def _():
        wt = w_ref[...]
        for h in range(IDX_H):
            wb_ref[h] = jnp.broadcast_to(wt[:, SM_WI + h:SM_WI + h + 1], (QBLK, 128))
        qi2 = qi_ref[...].reshape(IDX_H * QBLK, IDX_D)
        row = i * QBLK + _iota2((QBLK, 128), 0)
        lane = _iota2((QBLK, 128), 1)

        def score_tile(j, carry):
            off = pl.multiple_of(j * TK, TK)
            lg = _dot(qi2, kit_ref[:, pl.ds(off, TK)])
            for c in range(NC):
                s = jnp.zeros((QBLK, 128), F32)
                for h in range(IDX_H):
                    s = s + wb_ref[h] * jnp.maximum(
                        lg[h * QBLK:(h + 1) * QBLK, c * 128:(c + 1) * 128], 0.0)
                s = jnp.where(off + c * 128 + lane <= row, s, -jnp.inf)
                bits = lax.bitcast_convert_type(s, jnp.int32)
                key = bits ^ ((bits >> 31) & jnp.int32(0x7FFFFFFF))
                keys_ref[:, pl.ds(pl.multiple_of(off + c * 128, 128), 128)] = key
            return carry

        lax.fori_loop(0, n_t, score_tile, 0)

        def bit_step(b, lo):
            cand = lo + (jnp.int32(1) << (31 - b))
            cand_b = jnp.broadcast_to(cand, (QBLK, 128))

            def count_tile(j, acc):
                off = pl.multiple_of(j * TK, TK)
                for c in range(NC):
                    kt = keys_ref[:, pl.ds(pl.multiple_of(off + c * 128, 128), 128)]
                    acc = acc + jnp.where(kt >= cand_b, 1, 0)
                return acc

            acc = lax.fori_loop(0, n_t, count_tile, jnp.zeros((QBLK, 128), jnp.int32))
            cnt = jnp.sum(acc, axis=1, keepdims=True)
            return jnp.where(cnt >= KSEL, cand, lo)

        lo = lax.fori_loop(0, 32, bit_step, jnp.full((QBLK, 1), INT_MIN, jnp.int32))
        thr = jnp.maximum(lo, KEY_NEG_INF + 1)
        thr_ref[...] = jnp.broadcast_to(thr, (QBLK, 128))

    m_ref[...] = jnp.full_like(m_ref, NEG_BIG)
    l_ref[...] = jnp.zeros_like(l_ref)
    acc_ref[...] = jnp.zeros_like(acc_ref)
    qg = qb_ref[...].reshape(GH * QBLK, HEAD_DIM)
    thr_b = thr_ref[...]

    def attn_tile(j, carry):
        off = pl.multiple_of(j * TK, TK)
        s = _dot(qg, kt_ref[:, pl.ds(off, TK)])
        sel = jnp.concatenate(
            [keys_ref[:, pl.ds(pl.multiple_of(off + c * 128, 128), 128)] >= thr_b
             for c in range(NC)], axis=1)
        for hh in range(GH):
            sh = jnp.where(sel, s[hh * QBLK:(hh + 1) * QBLK, :], NEG_BIG)
            m_old = m_ref[hh]
            m_new = jnp.maximum(m_old, jnp.max(sh, axis=-1, keepdims=True))
            p = jnp.exp(sh - m_new)
            alpha = jnp.exp(m_old - m_new)
            l_ref[hh] = alpha * l_ref[hh] + jnp.sum(p, axis=-1, keepdims=True)
            acc_ref[hh] = alpha * acc_ref[hh]
            m_ref[hh] = m_new
            pbuf_ref[hh * QBLK:(hh + 1) * QBLK, :] = p.astype(BF16)
        pv = _dot(pbuf_ref[...], vb_ref[pl.ds(off, TK), :])
        acc_ref[...] = acc_ref[...] + pv.reshape(GH, QBLK, HEAD_DIM)
        return carry

    lax.fori_loop(0, n_t, attn_tile, 0)
    for hh in range(GH):
        out = acc_ref[hh] / l_ref[hh]
        sl = slice(hh * HEAD_DIM, (hh + 1) * HEAD_DIM)
        y_ref[:, sl] = (out * _silu(z_ref[:, sl])).astype(BF16)


def _dsa(proj, qb, qi, w, kit, kt, vb, ksel, tk=512):
    s = proj.shape[0]
    tk = min(tk, s)
    gh = H_B // HKV_B
    return pl.pallas_call(
        functools.partial(_dsa_kernel, TK=tk, KSEL=ksel),
        grid=(s // QBLK, HKV_B),
        in_specs=[pl.BlockSpec((gh, QBLK, HEAD_DIM), lambda i, g: (g, i, 0)),
                  pl.BlockSpec((IDX_H, QBLK, IDX_D), lambda i, g: (0, i, 0)),
                  pl.BlockSpec((QBLK, 128), lambda i, g: (i, 0)),
                  pl.BlockSpec((IDX_D, s), lambda i, g: (0, 0)),
                  pl.BlockSpec((HEAD_DIM, s), lambda i, g: (g, 0)),
                  pl.BlockSpec((s, HEAD_DIM), lambda i, g: (0, g)),
                  pl.BlockSpec((QBLK, gh * HEAD_DIM), lambda i, g: (i, OFF_BZ // (gh * HEAD_DIM) + g))],
        out_specs=pl.BlockSpec((QBLK, gh * HEAD_DIM), lambda i, g: (i, g)),
        out_shape=jax.ShapeDtypeStruct((s, H_B * HEAD_DIM), BF16),
        scratch_shapes=[pltpu.VMEM((QBLK, s), jnp.int32),
                        pltpu.VMEM((IDX_H, QBLK, 128), F32),
                        pltpu.VMEM((QBLK, 128), jnp.int32),
                        pltpu.VMEM((gh * QBLK, tk), BF16),
                        pltpu.VMEM((gh, QBLK, 1), F32),
                        pltpu.VMEM((gh, QBLK, 1), F32),
                        pltpu.VMEM((gh, QBLK, HEAD_DIM), F32)],
        compiler_params=_params(("arbitrary", "arbitrary")),
        name="dsa",
    )(qb, qi, w, kit, kt, vb, proj)


def _permute_w_in(w):
    sp = np.cumsum([0, 512, 512, 1024, 8, 8, 1024, 1024,
                    896, 512, 512, 64, 16, 2048,
                    3072, 8, 8, 1024])
    names = ["aq", "ak", "av", "ai", "af", "ao", "az", "bcq", "bk", "bv", "bki", "bwi", "bz",
             "cqkv", "cb", "ca", "cz"]
    c = {n: w[:, int(sp[t]):int(sp[t + 1])] for t, n in enumerate(names)}
    pad = jnp.zeros((w.shape[0], 16), w.dtype)
    cols = [c["bcq"], c["bki"], c["ai"], c["af"], c["bwi"], c["cb"], c["ca"], pad,
            c["aq"], c["ak"], c["av"], c["ao"], c["az"], c["cqkv"], c["cz"],
            c["bk"], c["bv"], c["bz"]]
    return jnp.concatenate(cols, axis=1).astype(BF16)


def _small_row(vals, off):
    row = jnp.zeros((128,), F32)
    return row.at[off:off + vals.shape[0]].set(vals.astype(F32))


def kernel(x, norm_g, w_in, w_out, m_ib, m_fb, m_norm_g, a_cq_g, a_wuq, a_wuqi, a_qn_g, a_kn_g,
           a_ki_g, a_ki_b, d_conv_w, d_a_log, d_dt_bias, d_norm_g):
    b, s, d = x.shape
    assert b == 1
    depth = w_in.shape[0]
    ksel = min(TOPK, s // 4)
    xs = x.reshape(s, d)
    for l in range(depth):
        w_r = _permute_w_in(w_in[l])
        wo = w_out[l].astype(BF16)
        gate_bias = (_small_row(m_ib[l], SM_AI) + _small_row(m_fb[l], SM_AF)).reshape(1, 128)
        arow = _small_row(d_a_log[l], SM_CA)
        dtrow = _small_row(d_dt_bias[l], SM_CA)
        wuqi_r = a_wuqi[l].reshape(Q_LORA, IDX_H, IDX_D).transpose(1, 0, 2).astype(BF16)

        h = _rmsnorm(xs, norm_g[l])
        proj = _in_proj(h, w_r)
        ya = _mlstm(proj, gate_bias, m_norm_g[l].reshape(1, -1))
        qb, qi, w, kt, vb, kit = _dsa_prep(
            proj, a_cq_g[l].reshape(1, -1), a_wuq[l].astype(BF16), wuqi_r,
            a_qn_g[l].reshape(1, -1), a_kn_g[l].reshape(1, -1),
            _small_row(a_ki_g[l], SM_KI).reshape(1, 128), _small_row(a_ki_b[l], SM_KI).reshape(1, 128))
        yb = _dsa(proj, qb, qi, w, kit, kt, vb, ksel)
        yc = _gdn(proj, d_conv_w[l], arow.reshape(1, 128), arow.reshape(128, 1),
                  dtrow.reshape(1, 128), dtrow.reshape(128, 1), d_norm_g[l].reshape(1, -1))
        xs = _out_proj(xs, ya, yb, yc, wo[0:1024], wo[1024:3072], wo[3072:4096])
    return xs.reshape(b, s, d)
```

```python
import functools

import jax
import jax.numpy as jnp
import numpy as np
from jax import lax
from jax.experimental import pallas as pl
from jax.experimental.pallas import tpu as pltpu

F32 = jnp.float32
BF16 = jnp.bfloat16
HI = lax.Precision.HIGHEST

HEAD_DIM = 128
H_A = 8
DQK_A = 64
GATE_CAP = 15.0
H_B = 16
HKV_B = 4
Q_LORA = 896
IDX_H = 16
IDX_D = 64
TOPK = 256
QBLK = 128
H_C = 8
CONV_K = 4
EPS = 1e-6

OFF_BCQ = 0
OFF_SMALL = 896
OFF_AQ = 1024
OFF_AK = 1536
OFF_AV = 2048
OFF_AO = 3072
OFF_AZ = 4096
OFF_CQ = 5120
OFF_CK = 6144
OFF_CV = 7168
OFF_CZ = 8192
OFF_BK = 9216
OFF_BV = 9728
OFF_BZ = 10240
N_PROJ = 12288
SM_KI = 0
SM_AI = 64
SM_AF = 72
SM_WI = 80
SM_CB = 96
SM_CA = 104

NEG_BIG = -1e30
LOG2E = 1.4426950408889634
KEY_NEG_INF = -2139095041

VMEM_LIMIT = 56 * 1024 * 1024


def _dot(a, b, prec=None):
    return jnp.dot(a, b, preferred_element_type=F32, precision=prec)


def _dot_nt(a, b, prec=None):
    return lax.dot_general(a, b, (((1,), (1,)), ((), ())), preferred_element_type=F32,
                           precision=prec)


def _dot_tn(a, b, prec=None):
    return lax.dot_general(a, b, (((0,), (0,)), ((), ())), preferred_element_type=F32,
                           precision=prec)


def _iota2(shape, dim):
    return lax.broadcasted_iota(jnp.int32, shape, dim)


def _log_sigmoid(x):
    return jnp.minimum(x, 0.0) - jnp.log(1.0 + jnp.exp(-jnp.abs(x)))


def _softplus(x):
    return jnp.maximum(x, 0.0) + jnp.log(1.0 + jnp.exp(-jnp.abs(x)))


def _soft_cap(x):
    return GATE_CAP * jnp.tanh(x / GATE_CAP)


def _silu(x):
    return x * jax.nn.sigmoid(x)


def _params(sem):
    return pltpu.CompilerParams(dimension_semantics=sem, vmem_limit_bytes=VMEM_LIMIT)


def _rmsnorm_kernel(x_ref, g_ref, o_ref):
    x = x_ref[...]
    ms = jnp.mean(x * x, axis=-1, keepdims=True)
    o_ref[...] = (x * lax.rsqrt(ms + EPS) * g_ref[...]).astype(o_ref.dtype)


def _rmsnorm(x, g, tm=512):
    s, d = x.shape
    return pl.pallas_call(
        _rmsnorm_kernel,
        grid=(s // tm,),
        in_specs=[pl.BlockSpec((tm, d), lambda i: (i, 0)),
                  pl.BlockSpec((1, d), lambda i: (0, 0))],
        out_specs=pl.BlockSpec((tm, d), lambda i: (i, 0)),
        out_shape=jax.ShapeDtypeStruct((s, d), BF16),
        compiler_params=_params(("parallel",)),
        name="rmsnorm",
    )(x, g.reshape(1, d))


def _matmul_kernel(a_ref, b_ref, o_ref):
    o_ref[...] = _dot(a_ref[...], b_ref[...])


def _in_proj(h, w, tm=1024, tn=512):
    s, d = h.shape
    n = w.shape[1]
    tm = min(tm, s)
    return pl.pallas_call(
        _matmul_kernel,
        grid=(s // tm, n // tn),
        in_specs=[pl.BlockSpec((tm, d), lambda i, j: (i, 0)),
                  pl.BlockSpec((d, tn), lambda i, j: (0, j))],
        out_specs=pl.BlockSpec((tm, tn), lambda i, j: (i, j)),
        out_shape=jax.ShapeDtypeStruct((s, n), F32),
        compiler_params=_params(("parallel", "arbitrary")),
        name="in_proj",
    )(h, w)


def _out_proj_kernel(x_ref, ya_ref, yb_ref, yc_ref, wa_ref, wb_ref, wc_ref, o_ref):
    acc = _dot(ya_ref[...], wa_ref[...])
    acc = acc + _dot(yb_ref[...], wb_ref[...])
    acc = acc + _dot(yc_ref[...], wc_ref[...])
    o_ref[...] = x_ref[...] + acc


def _out_proj(x, ya, yb, yc, wa, wb, wc, tm=1024, tn=512):
    s, d = x.shape
    tm = min(tm, s)
    na, nb, nc = ya.shape[1], yb.shape[1], yc.shape[1]
    return pl.pallas_call(
        _out_proj_kernel,
        grid=(s // tm, d // tn),
        in_specs=[pl.BlockSpec((tm, tn), lambda i, j: (i, j)),
                  pl.BlockSpec((tm, na), lambda i, j: (i, 0)),
                  pl.BlockSpec((tm, nb), lambda i, j: (i, 0)),
                  pl.BlockSpec((tm, nc), lambda i, j: (i, 0)),
                  pl.BlockSpec((na, tn), lambda i, j: (0, j)),
                  pl.BlockSpec((nb, tn), lambda i, j: (0, j)),
                  pl.BlockSpec((nc, tn), lambda i, j: (0, j))],
        out_specs=pl.BlockSpec((tm, tn), lambda i, j: (i, j)),
        out_shape=jax.ShapeDtypeStruct((s, d), F32),
        compiler_params=_params(("parallel", "arbitrary")),
        name="out_proj",
    )(x, ya, yb, yc, wa, wb, wc)


def _mlstm_kernel(q_ref, k_ref, v_ref, o_ref, z_ref, sm_ref, bias_ref, ng_ref, y_ref,
                  c_ref, n_ref, m_ref, *, L):
    @pl.when(pl.program_id(0) == 0)
    def _():
        c_ref[...] = jnp.zeros_like(c_ref)
        n_ref[...] = jnp.zeros_like(n_ref)
        m_ref[...] = jnp.zeros_like(m_ref)

    sm = sm_ref[...] + bias_ref[...]
    gcol = _soft_cap(sm)
    lscol = _log_sigmoid(gcol)
    grow = _soft_cap(sm.T[SM_AI:SM_AI + 16, :])
    lsrow = _log_sigmoid(grow)
    ri = _iota2((L, L), 0)
    ci = _iota2((L, L), 1)
    causal = ci <= ri
    tril = jnp.where(causal, 1.0, 0.0).astype(F32)
    triu = jnp.where(ri <= ci, 1.0, 0.0).astype(F32)
    bcol = _dot(tril, lscol, HI)
    brow = _dot(lsrow, triu, HI)
    scale = DQK_A ** -0.5

    for h in range(H_A):
        b_c = bcol[:, SM_AF + h:SM_AF + h + 1]
        li_c = gcol[:, SM_AI + h:SM_AI + h + 1]
        b_r = brow[8 + h:9 + h, :]
        li_r = grow[h:h + 1, :]
        b_last = b_r[:, L - 1:L]
        m_st = m_ref[h:h + 1, 0:1]
        qh = (q_ref[:, h * DQK_A:(h + 1) * DQK_A] * scale).astype(BF16)
        kf = k_ref[:, h * DQK_A:(h + 1) * DQK_A]
        kh = kf.astype(BF16)
        vh = v_ref[:, h * HEAD_DIM:(h + 1) * HEAD_DIM].astype(BF16)
        c_st = c_ref[h]
        n_st = n_ref[h:h + 1, :]

        d_log = jnp.where(causal, b_c - b_r + li_r, -jnp.inf)
        inter = b_c + m_st
        m_j = jnp.maximum(inter, jnp.max(d_log, axis=-1, keepdims=True))
        p = jnp.exp(d_log - m_j) * _dot_nt(qh, kh)
        w_inter = jnp.exp(inter - m_j)
        num = _dot(p.astype(BF16), vh) + w_inter * _dot(qh, c_st.astype(BF16))
        qn = jnp.sum(qh.astype(F32) * n_st, axis=-1, keepdims=True)
        den = jnp.sum(p, axis=-1, keepdims=True) + w_inter * qn
        h_out = num / jnp.maximum(jnp.abs(den), jnp.exp(-m_j))

        u_c = b_last - b_c + li_c
        m_new = jnp.maximum(b_last + m_st, jnp.max(u_c, axis=0, keepdims=True))
        w_u = jnp.exp(u_c - m_new)
        decay = jnp.exp(b_last + m_st - m_new)
        kw = kf * w_u
        c_ref[h] = decay * c_st + _dot_tn(kw.astype(BF16), vh)
        n_ref[h:h + 1, :] = decay * n_st + jnp.sum(kw, axis=0, keepdims=True)
        m_ref[h:h + 1, :] = jnp.broadcast_to(m_new, (1, 128))

        g = ng_ref[:, h * HEAD_DIM:(h + 1) * HEAD_DIM]
        hn = h_out * lax.rsqrt(jnp.mean(h_out * h_out, axis=-1, keepdims=True) + EPS) * g
        oh = o_ref[:, h * HEAD_DIM:(h + 1) * HEAD_DIM]
        zh = z_ref[:, h * HEAD_DIM:(h + 1) * HEAD_DIM]
        y_ref[:, h * HEAD_DIM:(h + 1) * HEAD_DIM] = (jax.nn.sigmoid(oh) * hn * _silu(zh)).astype(BF16)


def _mlstm(proj, bias_row, norm_g, L=256):
    s = proj.shape[0]
    L = min(L, s)
    return pl.pallas_call(
        functools.partial(_mlstm_kernel, L=L),
        grid=(s // L,),
        in_specs=[pl.BlockSpec((L, 512), lambda i: (i, OFF_AQ // 512)),
                  pl.BlockSpec((L, 512), lambda i: (i, OFF_AK // 512)),
                  pl.BlockSpec((L, 1024), lambda i: (i, OFF_AV // 1024)),
                  pl.BlockSpec((L, 1024), lambda i: (i, OFF_AO // 1024)),
                  pl.BlockSpec((L, 1024), lambda i: (i, OFF_AZ // 1024)),
                  pl.BlockSpec((L, 128), lambda i: (i, OFF_SMALL // 128)),
                  pl.BlockSpec((1, 128), lambda i: (0, 0)),
                  pl.BlockSpec((1, 1024), lambda i: (0, 0))],
        out_specs=pl.BlockSpec((L, 1024), lambda i: (i, 0)),
        out_shape=jax.ShapeDtypeStruct((s, 1024), BF16),
        scratch_shapes=[pltpu.VMEM((H_A, DQK_A, HEAD_DIM), F32),
                        pltpu.VMEM((H_A, DQK_A), F32),
                        pltpu.VMEM((H_A, 128), F32)],
        compiler_params=_params(("arbitrary",)),
        name="mlstm",
    )(proj, proj, proj, proj, proj, proj, bias_row, norm_g)


GC = 64


def _split_bf16(a):
    hi = a.astype(BF16)
    lo = (a - hi.astype(F32)).astype(BF16)
    return hi, lo


def _dot3(a, b):
    ah, al = _split_bf16(a)
    bh, bl = _split_bf16(b)
    return _dot(ah, bh) + (_dot(ah, bl) + _dot(al, bh))


def _gdn_prep_kernel(q_ref, k_ref, v_ref, hq_ref, hk_ref, hv_ref, sm_ref, cw_ref,
                     arow_ref, acol_ref, dtrow_ref, dtcol_ref,
                     u_ref, w_ref, qd_ref, kd_ref, aqk_ref, dec_ref):
    i = pl.program_id(0)

    def conv_silu(x_ref, h_ref, w):
        halo = jnp.where(i > 0, h_ref[...], 0.0)
        xf = jnp.concatenate([halo, x_ref[...]], axis=0)
        y = w[3:4, :] * xf[8:8 + GC]
        for sh in range(1, CONV_K):
            y = y + w[3 - sh:4 - sh, :] * pltpu.roll(xf, sh, 0)[8:8 + GC]
        return _silu(y)

    qc = conv_silu(q_ref, hq_ref, cw_ref[:, 0:1024])
    kc = conv_silu(k_ref, hk_ref, cw_ref[:, 1024:2048])
    vc = conv_silu(v_ref, hv_ref, cw_ref[:, 2048:3072])

    sm = sm_ref[...]
    smT = sm.T
    beta_col = jax.nn.sigmoid(sm)
    g_col = -jnp.exp(arow_ref[...]) * _softplus(sm + dtrow_ref[...])
    g_row = -jnp.exp(acol_ref[SM_CA:SM_CA + 8, :]) * _softplus(
        smT[SM_CA:SM_CA + 8, :] + dtcol_ref[SM_CA:SM_CA + 8, :])
    ri = _iota2((GC, GC), 0)
    ci = _iota2((GC, GC), 1)
    causal = ci <= ri
    strict = ci < ri
    bdiag = (ri // 16) == (ci // 16)
    eye = jnp.where(ri == ci, 1.0, 0.0).astype(F32)
    tril = jnp.where(causal, 1.0, 0.0).astype(F32)
    triu = jnp.where(ri <= ci, 1.0, 0.0).astype(F32)
    gc_col = _dot(tril, g_col, HI)
    gc_row = _dot(g_row, triu, HI)
    scale = HEAD_DIM ** -0.5

    heads = range(H_C)
    sls = [slice(h * HEAD_DIM, (h + 1) * HEAD_DIM) for h in heads]

    def each(fn, *lists):
        return [fn(*args) for args in zip(*lists)]

    qh = [qc[:, sl] for sl in sls]
    kh = [kc[:, sl] for sl in sls]
    vh = [vc[:, sl] for sl in sls]
    qh = each(lambda t: t * lax.rsqrt(jnp.sum(t * t, axis=-1, keepdims=True) + EPS) * scale, qh)
    kh = each(lambda t: t * lax.rsqrt(jnp.sum(t * t, axis=-1, keepdims=True) + EPS), kh)
    gc_c = [gc_col[:, SM_CA + h:SM_CA + h + 1] for h in heads]
    gc_r = [gc_row[h:h + 1, :] for h in heads]
    beta_c = [beta_col[:, SM_CB + h:SM_CB + h + 1] for h in heads]
    g_last = [t[GC - 1:GC, :] for t in gc_c]
    e_gc = each(jnp.exp, gc_c)
    gamma = each(lambda c, r: jnp.where(causal, jnp.exp(jnp.where(causal, c - r, 0.0)), 0.0),
                 gc_c, gc_r)
    kb = each(lambda k, b: k * b, kh, beta_c)
    kh_b = each(lambda k: k.astype(BF16), kh)
    a = each(lambda kbh, kbf, gm: jnp.where(strict, _dot_nt(kbh.astype(BF16), kbf) * gm, 0.0),
             kb, kh_b, gamma)
    ad = each(lambda t: jnp.where(bdiag, t, 0.0), a)
    ao = each(lambda t, d: t - d, a, ad)
    a2 = each(_dot3, ad, ad)
    a4 = each(_dot3, a2, a2)
    a8 = each(_dot3, a4, a4)
    dinv = each(lambda d: eye - d, ad)
    for pw in (a2, a4, a8):
        dinv = each(lambda d, p: d + _dot3(d, p), dinv, pw)
    nn = each(_dot3, dinv, ao)
    n2 = each(_dot3, nn, nn)
    ninv = each(lambda n: eye - n, nn)
    ninv = each(lambda n, p: n + _dot3(n, p), ninv, n2)
    tinv = each(lambda n, d: _dot3(n, d).astype(BF16), ninv, dinv)
    u = each(lambda t, v, b: _dot(t, (v * b).astype(BF16)), tinv, vh, beta_c)
    w = each(lambda t, k, e: _dot(t, (k * e).astype(BF16)), tinv, kb, e_gc)
    aqk = each(lambda q, kbf, gm: _dot_nt(q.astype(BF16), kbf) * gm, qh, kh_b, gamma)
    for h in heads:
        sl = sls[h]
        u_ref[:, sl] = u[h]
        w_ref[:, sl] = w[h].astype(BF16)
        qd_ref[:, sl] = (qh[h] * e_gc[h]).astype(BF16)
        kd_ref[:, sl] = (kh[h] * jnp.exp(g_last[h] - gc_c[h])).astype(BF16)
        aqk_ref[h] = aqk[h].astype(BF16)
        dec_ref[0, h:h + 1, :] = jnp.broadcast_to(jnp.exp(g_last[h]), (1, 128))


def _gdn_prep(proj, conv_w, arow, acol, dtrow, dtcol):
    s = proj.shape[0]
    nch = s // GC

    def halo(col):
        return pl.BlockSpec((8, 1024), lambda i: (jnp.maximum(i * (GC // 8) - 1, 0), col))

    row = lambda i: (i, 0)
    return pl.pallas_call(
        _gdn_prep_kernel,
        grid=(nch,),
        in_specs=[pl.BlockSpec((GC, 1024), lambda i: (i, OFF_CQ // 1024)),
                  pl.BlockSpec((GC, 1024), lambda i: (i, OFF_CK // 1024)),
                  pl.BlockSpec((GC, 1024), lambda i: (i, OFF_CV // 1024)),
                  halo(OFF_CQ // 1024), halo(OFF_CK // 1024), halo(OFF_CV // 1024),
                  pl.BlockSpec((GC, 128), lambda i: (i, OFF_SMALL // 128)),
                  pl.BlockSpec((CONV_K, 3072), lambda i: (0, 0)),
                  pl.BlockSpec((1, 128), lambda i: (0, 0)),
                  pl.BlockSpec((128, 1), lambda i: (0, 0)),
                  pl.BlockSpec((1, 128), lambda i: (0, 0)),
                  pl.BlockSpec((128, 1), lambda i: (0, 0))],
        out_specs=[pl.BlockSpec((GC, 1024), row), pl.BlockSpec((GC, 1024), row),
                   pl.BlockSpec((GC, 1024), row), pl.BlockSpec((GC, 1024), row),
                   pl.BlockSpec((H_C, GC, GC), lambda i: (0, i, 0)),
                   pl.BlockSpec((1, H_C, 128), lambda i: (i, 0, 0))],
        out_shape=[jax.ShapeDtypeStruct((s, 1024), F32),
                   jax.ShapeDtypeStruct((s, 1024), BF16),
                   jax.ShapeDtypeStruct((s, 1024), BF16),
                   jax.ShapeDtypeStruct((s, 1024), BF16),
                   jax.ShapeDtypeStruct((H_C, s, GC), BF16),
                   jax.ShapeDtypeStruct((nch, H_C, 128), F32)],
        compiler_params=_params(("parallel",)),
        name="gdn_prep",
    )(proj, proj, proj, proj, proj, proj, proj, conv_w, arow, acol, dtrow, dtcol)


def _gdn_scan_kernel(u_ref, w_ref, qd_ref, kd_ref, aqk_ref, dec_ref, z_ref, ng_ref, y_ref,
                     s_ref, *, NCH):
    @pl.when(pl.program_id(0) == 0)
    def _():
        s_ref[...] = jnp.zeros_like(s_ref)

    heads = range(H_C)
    sls = [slice(h * HEAD_DIM, (h + 1) * HEAD_DIM) for h in heads]
    for c in range(NCH):
        rs = slice(c * GC, (c + 1) * GC)
        st = [s_ref[h] for h in heads]
        st_b = [t.astype(BF16) for t in st]
        v_new = [(u_ref[rs, sls[h]] - _dot(w_ref[rs, sls[h]], st_b[h])).astype(BF16)
                 for h in heads]
        for h in heads:
            s_ref[h] = dec_ref[c, h:h + 1, :] * st[h] + _dot_tn(kd_ref[rs, sls[h]], v_new[h])
        o = [_dot(qd_ref[rs, sls[h]], st_b[h]) + _dot(aqk_ref[h, rs, :], v_new[h])
             for h in heads]
        for h in heads:
            on = o[h] * lax.rsqrt(jnp.mean(o[h] * o[h], axis=-1, keepdims=True) + EPS) * ng_ref[...]
            y_ref[rs, sls[h]] = (on * _silu(z_ref[rs, sls[h]])).astype(BF16)


def _gdn_scan(proj, u, w, qd, kd, aqk, dec, norm_g, nch=4):
    s = proj.shape[0]
    tt = nch * GC
    row = lambda i: (i, 0)
    return pl.pallas_call(
        functools.partial(_gdn_scan_kernel, NCH=nch),
        grid=(s // tt,),
        in_specs=[pl.BlockSpec((tt, 1024), row), pl.BlockSpec((tt, 1024), row),
                  pl.BlockSpec((tt, 1024), row), pl.BlockSpec((tt, 1024), row),
                  pl.BlockSpec((H_C, tt, GC), lambda i: (0, i, 0)),
                  pl.BlockSpec((nch, H_C, 128), lambda i: (i, 0, 0)),
                  pl.BlockSpec((tt, 1024), lambda i: (i, OFF_CZ // 1024)),
                  pl.BlockSpec((1, 128), lambda i: (0, 0))],
        out_specs=pl.BlockSpec((tt, 1024), row),
        out_shape=jax.ShapeDtypeStruct((s, 1024), BF16),
        scratch_shapes=[pltpu.VMEM((H_C, HEAD_DIM, HEAD_DIM), F32)],
        compiler_params=_params(("arbitrary",)),
        name="gdn_scan",
    )(u, w, qd, kd, aqk, dec, proj, norm_g)


def _dsa_prep_kernel(cq_ref, k_ref, v_ref, sm_ref, cqg_ref, wuq_ref, wuqi_ref, qng_ref,
                     kng_ref, kig_ref, kib_ref,
                     qb_ref, qi_ref, w_ref, kt_ref, vb_ref, kit_ref):
    cq = cq_ref[...]
    cq = cq * lax.rsqrt(jnp.mean(cq * cq, axis=-1, keepdims=True) + EPS) * cqg_ref[...]
    cq = cq.astype(BF16)
    qf = _dot(cq, wuq_ref[...])
    qscale = HEAD_DIM ** -0.5 * LOG2E
    for h in range(H_B):
        qh = qf[:, h * HEAD_DIM:(h + 1) * HEAD_DIM]
        qh = qh * lax.rsqrt(jnp.mean(qh * qh, axis=-1, keepdims=True) + EPS) * qng_ref[...]
        qb_ref[h] = (qh * qscale).astype(BF16)
        qi_ref[h] = _dot(cq, wuqi_ref[h]).astype(BF16)
    kn = []
    for g in range(HKV_B):
        kg = k_ref[:, g * HEAD_DIM:(g + 1) * HEAD_DIM]
        kn.append(kg * lax.rsqrt(jnp.mean(kg * kg, axis=-1, keepdims=True) + EPS) * kng_ref[...])
    kt_ref[...] = jnp.concatenate(kn, axis=1).T.astype(BF16)
    vb_ref[...] = v_ref[...].astype(BF16)
    sm = sm_ref[...]
    w_ref[...] = sm * (IDX_H ** -0.5 * IDX_D ** -0.5)
    lane = _iota2(sm.shape, 1)
    is_ki = lane < IDX_D
    mu = jnp.sum(jnp.where(is_ki, sm, 0.0), axis=-1, keepdims=True) * (1.0 / IDX_D)
    xc = jnp.where(is_ki, sm - mu, 0.0)
    var = jnp.sum(xc * xc, axis=-1, keepdims=True) * (1.0 / IDX_D)
    ki = xc * lax.rsqrt(var + EPS) * kig_ref[...] + kib_ref[...]
    kit_ref[...] = ki.T[0:IDX_D, :].astype(BF16)


def _dsa_prep(proj, cq_g, wuq, wuqi_r, qn_g, kn_g, ki_g, ki_b, tt=256):
    s = proj.shape[0]
    tt = min(tt, s)
    const2 = lambda i: (0, 0)
    return pl.pallas_call(
        _dsa_prep_kernel,
        grid=(s // tt,),
        in_specs=[pl.BlockSpec((tt, Q_LORA), lambda i: (i, 0)),
                  pl.BlockSpec((tt, 512), lambda i: (i, OFF_BK // 512)),
                  pl.BlockSpec((tt, 512), lambda i: (i, OFF_BV // 512)),
                  pl.BlockSpec((tt, 128), lambda i: (i, OFF_SMALL // 128)),
                  pl.BlockSpec((1, Q_LORA), const2),
                  pl.BlockSpec((Q_LORA, H_B * HEAD_DIM), const2),
                  pl.BlockSpec((IDX_H, Q_LORA, IDX_D), lambda i: (0, 0, 0)),
                  pl.BlockSpec((1, HEAD_DIM), const2),
                  pl.BlockSpec((1, HEAD_DIM), const2),
                  pl.BlockSpec((1, 128), const2),
                  pl.BlockSpec((1, 128), const2)],
        out_specs=[pl.BlockSpec((H_B, tt, HEAD_DIM), lambda i: (0, i, 0)),
                   pl.BlockSpec((IDX_H, tt, IDX_D), lambda i: (0, i, 0)),
                   pl.BlockSpec((tt, 128), lambda i: (i, 0)),
                   pl.BlockSpec((HKV_B * HEAD_DIM, tt), lambda i: (0, i)),
                   pl.BlockSpec((tt, HKV_B * HEAD_DIM), lambda i: (i, 0)),
                   pl.BlockSpec((IDX_D, tt), lambda i: (0, i))],
        out_shape=[jax.ShapeDtypeStruct((H_B, s, HEAD_DIM), BF16),
                   jax.ShapeDtypeStruct((IDX_H, s, IDX_D), BF16),
                   jax.ShapeDtypeStruct((s, 128), F32),
                   jax.ShapeDtypeStruct((HKV_B * HEAD_DIM, s), BF16),
                   jax.ShapeDtypeStruct((s, HKV_B * HEAD_DIM), BF16),
                   jax.ShapeDtypeStruct((IDX_D, s), BF16)],
        compiler_params=_params(("parallel",)),
        name="dsa_prep",
    )(proj, proj, proj, proj, cq_g, wuq, wuqi_r, qn_g, kn_g, ki_g, ki_b)


TKS = 512


def _key_of(f):
    bits = lax.bitcast_convert_type(f, jnp.int32)
    return bits ^ ((bits >> 31) & jnp.int32(0x7FFFFFFF))


def _float_of(k):
    return lax.bitcast_convert_type(k ^ ((k >> 31) & jnp.int32(0x7FFFFFFF)), F32)


def _dsa_kernel(qb_ref, qi_ref, w_ref, kit_ref, kt_ref, vb_ref, z_ref, y_ref,
                keys_ref, bias_ref, wb_ref, qaug_ref, kaug_ref, vaug_ref, sbuf_ref, pbuf_ref,
                alpha_ref, m_ref, acc_ref, *, TK, KSEL):
    i = pl.program_id(0)
    g = pl.program_id(1)
    n_t = ((i + 1) * QBLK + TK - 1) // TK
    NS = TK // TKS
    assert NS == 2
    GH = H_B // HKV_B

    def key_chunk(off, c):
        return keys_ref[:, pl.ds(pl.multiple_of(off + c * 128, 128), 128)]

    @pl.when(g == 0)
    def _():
        wt = w_ref[...]
        for h in range(IDX_H):
            wb_ref[h] = jnp.broadcast_to(wt[:, SM_WI + h:SM_WI + h + 1], (QBLK, 128))
        qi2 = qi_ref[...].reshape(IDX_H * QBLK, IDX_D)
        row = i * QBLK + _iota2((QBLK, 128), 0)
        lane = _iota2((QBLK, 128), 1)

        def score_tile(j, carry):
            mx, mn = carry
            for sub in range(NS):
                off = pl.multiple_of(j * TK + sub * TKS, TKS)
                lg = _dot(qi2, kit_ref[:, pl.ds(off, TKS)])
                for c in range(TKS // 128):
                    s = jnp.zeros((QBLK, 128), F32)
                    for h in range(IDX_H):
                        s = s + wb_ref[h] * jnp.maximum(
                            lg[h * QBLK:(h + 1) * QBLK, c * 128:(c + 1) * 128], 0.0)
                    valid = off + c * 128 + lane <= row
                    mx = jnp.maximum(mx, jnp.where(valid, s, -jnp.inf))
                    mn = jnp.minimum(mn, jnp.where(valid, s, jnp.inf))
                    keys_ref[:, pl.ds(pl.multiple_of(off + c * 128, 128), 128)] = _key_of(
                        jnp.where(valid, s, -jnp.inf))
            return mx, mn

        mx, mn = lax.fori_loop(0, n_t, score_tile,
                               (jnp.full((QBLK, 128), -jnp.inf, F32),
                                jnp.full((QBLK, 128), jnp.inf, F32)))
        mx = jnp.max(mx, axis=1, keepdims=True)
        mn = jnp.min(mn, axis=1, keepdims=True)

        def count_ge(cand):
            cand_b = jnp.broadcast_to(cand, (QBLK, 128))

            def count_tile(j, acc):
                off = pl.multiple_of(j * TK, TK)
                for c in range(TK // 128):
                    acc = acc + jnp.where(key_chunk(off, c) >= cand_b, 1, 0)
                return acc

            acc = lax.fori_loop(0, n_t, count_tile, jnp.zeros((QBLK, 128), jnp.int32))
            return jnp.sum(acc, axis=1, keepdims=True)

        nvalid = i * QBLK + _iota2((QBLK, 1), 0) + 1
        done0 = nvalid <= KSEL
        log_k = float(np.log(KSEL))

        def cond(st):
            return jnp.logical_and(st[0] < 160, st[1] > 0)

        def body(st):
            it, _, lo, hi, clo, chi, thr, done_i, side = st
            done = done_i != 0
            adj = jnp.logical_and(hi == lo + 1, jnp.logical_not(done))
            thr = jnp.where(adj, lo, thr)
            done = jnp.logical_or(done, adj)
            lo_f = _float_of(lo)
            hi_f = _float_of(hi)
            d_lo = jnp.log(clo) - log_k
            d_hi = log_k - jnp.log(jnp.maximum(chi, 0.5))
            d_lo = jnp.where(side <= -2, 0.5 * d_lo, d_lo)
            d_hi = jnp.where(side >= 2, 0.5 * d_hi, d_hi)
            cand_i = _key_of(lo_f + (hi_f - lo_f) * (d_lo / (d_lo + d_hi)))
            mid = (lo >> 1) + (hi >> 1) + (lo & hi & 1)
            cand = jnp.where(it % 4 == 3, mid, cand_i)
            cand = jnp.minimum(jnp.maximum(cand, lo + 1), hi - 1)
            cand = jnp.where(done, thr, cand)
            cnt = count_ge(cand)
            ge = cnt >= KSEL
            eq = jnp.logical_and(cnt == KSEL, jnp.logical_not(done))
            thr = jnp.where(eq, cand, thr)
            done = jnp.logical_or(done, eq)
            cnt_f = cnt.astype(F32)
            lo = jnp.where(ge, cand, lo)
            clo = jnp.where(ge, cnt_f, clo)
            hi = jnp.where(ge, hi, cand)
            chi = jnp.where(ge, chi, cnt_f)
            side = jnp.where(ge, jnp.where(side > 0, side + 1, 1), jnp.where(side < 0, side - 1, -1))
            n_open = jnp.sum(jnp.where(done, 0, 1))
            return it + 1, n_open, lo, hi, clo, chi, thr, jnp.where(done, 1, 0), side

        st0 = (jnp.int32(0), jnp.sum(jnp.where(done0, 0, 1)),
               _key_of(mn), _key_of(mx) + 1, nvalid.astype(F32), jnp.zeros((QBLK, 1), F32),
               jnp.full((QBLK, 1), KEY_NEG_INF + 1, jnp.int32), jnp.where(done0, 1, 0),
               jnp.zeros((QBLK, 1), jnp.int32))
        st = lax.while_loop(cond, body, st0)
        thr = jnp.where(st[7] != 0, st[6], st[2])
        thr_b = jnp.broadcast_to(jnp.maximum(thr, KEY_NEG_INF + 1), (QBLK, 128))

        def bias_tile(j, carry):
            off = pl.multiple_of(j * TK, TK)
            for c in range(TK // 128):
                bias_ref[:, pl.ds(pl.multiple_of(off + c * 128, 128), 128)] = jnp.where(
                    key_chunk(off, c) >= thr_b, 0.0, NEG_BIG).astype(BF16)
            return carry

        lax.fori_loop(0, n_t, bias_tile, 0)

    m_ref[...] = jnp.full_like(m_ref, NEG_BIG)
    acc_ref[...] = jnp.zeros_like(acc_ref)
    qaug_ref[:, 0:HEAD_DIM] = qb_ref[...].reshape(GH * QBLK, HEAD_DIM)
    ri = _iota2((GH * QBLK, QBLK), 0)
    ci = _iota2((GH * QBLK, QBLK), 1)
    qaug_ref[:, HEAD_DIM:2 * HEAD_DIM] = jnp.where((ri % QBLK) == ci, 1.0, 0.0).astype(BF16)
    vaug_ref[:, :, HEAD_DIM:2 * HEAD_DIM] = jnp.ones((NS, TKS, HEAD_DIM), BF16)

    def scores(t, slot):
        off = pl.multiple_of(t * TKS, TKS)
        kaug_ref[slot, 0:HEAD_DIM, :] = kt_ref[:, pl.ds(off, TKS)]
        kaug_ref[slot, HEAD_DIM:2 * HEAD_DIM, :] = bias_ref[:, pl.ds(off, TKS)]
        sbuf_ref[slot] = _dot(qaug_ref[...], kaug_ref[slot])

    def softmax_pv(t, slot):
        off = pl.multiple_of(t * TKS, TKS)
        for hh in range(GH):
            rs = slice(hh * QBLK, (hh + 1) * QBLK)
            sh = sbuf_ref[slot, rs, :]
            m_old = m_ref[rs]
            m_new = jnp.maximum(m_old, jnp.max(sh, axis=-1, keepdims=True))
            pbuf_ref[slot, rs, :] = jnp.exp2(sh - m_new).astype(BF16)
            alpha_ref[slot, rs, :] = jnp.exp2(m_old - m_new)
            m_ref[rs] = m_new
        vaug_ref[slot, :, 0:HEAD_DIM] = vb_ref[pl.ds(off, TKS), :]
        pv = _dot(pbuf_ref[slot], vaug_ref[slot])
        acc_ref[...] = alpha_ref[slot] * acc_ref[...] + pv

    def attn_tile(j, carry):
        t0 = j * NS
        scores(t0 + 1, 1)
        softmax_pv(t0, 0)
        scores(t0 + 2, 0)
        softmax_pv(t0 + 1, 1)
        return carry

    scores(0, 0)
    lax.fori_loop(0, n_t - 1, attn_tile, 0)
    t_last = (n_t - 1) * NS
    scores(t_last + 1, 1)
    softmax_pv(t_last, 0)
    softmax_pv(t_last + 1, 1)
    for hh in range(GH):
        rs = slice(hh * QBLK, (hh + 1) * QBLK)
        out = acc_ref[rs, 0:HEAD_DIM] / acc_ref[rs, HEAD_DIM:2 * HEAD_DIM]
        sl = slice(hh * HEAD_DIM, (hh + 1) * HEAD_DIM)
        y_ref[:, sl] = (out * _silu(z_ref[:, sl])).astype(BF16)


def _dsa(proj, qb, qi, w, kit, kt, vb, ksel, tk=1024):
    s = proj.shape[0]
    tk = min(tk, s)
    ns = tk // TKS
    gh = H_B // HKV_B
    return pl.pallas_call(
        functools.partial(_dsa_kernel, TK=tk, KSEL=ksel),
        grid=(s // QBLK, HKV_B),
        in_specs=[pl.BlockSpec((gh, QBLK, HEAD_DIM), lambda i, g: (g, i, 0)),
                  pl.BlockSpec((IDX_H, QBLK, IDX_D), lambda i, g: (0, i, 0)),
                  pl.BlockSpec((QBLK, 128), lambda i, g: (i, 0)),
                  pl.BlockSpec((IDX_D, s), lambda i, g: (0, 0)),
                  pl.BlockSpec((HEAD_DIM, s), lambda i, g: (g, 0)),
                  pl.BlockSpec((s, HEAD_DIM), lambda i, g: (0, g)),
                  pl.BlockSpec((QBLK, gh * HEAD_DIM), lambda i, g: (i, OFF_BZ // (gh * HEAD_DIM) + g))],
        out_specs=pl.BlockSpec((QBLK, gh * HEAD_DIM), lambda i, g: (i, g)),
        out_shape=jax.ShapeDtypeStruct((s, H_B * HEAD_DIM), BF16),
        scratch_shapes=[pltpu.VMEM((QBLK, s), jnp.int32),
                        pltpu.VMEM((QBLK, s), BF16),
                        pltpu.VMEM((IDX_H, QBLK, 128), F32),
                        pltpu.VMEM((gh * QBLK, 2 * HEAD_DIM), BF16),
                        pltpu.VMEM((ns, 2 * HEAD_DIM, TKS), BF16),
                        pltpu.VMEM((ns, TKS, 2 * HEAD_DIM), BF16),
                        pltpu.VMEM((ns, gh * QBLK, TKS), F32),
                        pltpu.VMEM((ns, gh * QBLK, TKS), BF16),
                        pltpu.VMEM((ns, gh * QBLK, 1), F32),
                        pltpu.VMEM((gh * QBLK, 1), F32),
                        pltpu.VMEM((gh * QBLK, 2 * HEAD_DIM), F32)],
        compiler_params=_params(("arbitrary", "arbitrary")),
        name="dsa",
    )(qb, qi, w, kit, kt, vb, proj)


def _permute_w_in(w):
    sp = np.cumsum([0, 512, 512, 1024, 8, 8, 1024, 1024,
                    896, 512, 512, 64, 16, 2048,
                    3072, 8, 8, 1024])
    names = ["aq", "ak", "av", "ai", "af", "ao", "az", "bcq", "bk", "bv", "bki", "bwi", "bz",
             "cqkv", "cb", "ca", "cz"]
    c = {n: w[:, int(sp[t]):int(sp[t + 1])] for t, n in enumerate(names)}
    pad = jnp.zeros((w.shape[0], 16), w.dtype)
    cols = [c["bcq"], c["bki"], c["ai"], c["af"], c["bwi"], c["cb"], c["ca"], pad,
            c["aq"], c["ak"], c["av"], c["ao"], c["az"], c["cqkv"], c["cz"],
            c["bk"], c["bv"], c["bz"]]
    return jnp.concatenate(cols, axis=1).astype(BF16)


def _small_row(vals, off):
    row = jnp.zeros((128,), F32)
    return row.at[off:off + vals.shape[0]].set(vals.astype(F32))


def kernel(x, norm_g, w_in, w_out, m_ib, m_fb, m_norm_g, a_cq_g, a_wuq, a_wuqi, a_qn_g, a_kn_g,
           a_ki_g, a_ki_b, d_conv_w, d_a_log, d_dt_bias, d_norm_g):
    b, s, d = x.shape
    assert b == 1
    depth = w_in.shape[0]
    ksel = min(TOPK, s // 4)
    xs = x.reshape(s, d)
    for l in range(depth):
        w_r = _permute_w_in(w_in[l])
        wo = w_out[l].astype(BF16)
        gate_bias = (_small_row(m_ib[l], SM_AI) + _small_row(m_fb[l], SM_AF)).reshape(1, 128)
        arow = _small_row(d_a_log[l], SM_CA)
        dtrow = _small_row(d_dt_bias[l], SM_CA)
        wuqi_r = a_wuqi[l].reshape(Q_LORA, IDX_H, IDX_D).transpose(1, 0, 2).astype(BF16)

        h = _rmsnorm(xs, norm_g[l])
        proj = _in_proj(h, w_r)
        ya = _mlstm(proj, gate_bias, m_norm_g[l].reshape(1, -1))
        qb, qi, w, kt, vb, kit = _dsa_prep(
            proj, a_cq_g[l].reshape(1, -1), a_wuq[l].astype(BF16), wuqi_r,
            a_qn_g[l].reshape(1, -1), a_kn_g[l].reshape(1, -1),
            _small_row(a_ki_g[l], SM_KI).reshape(1, 128), _small_row(a_ki_b[l], SM_KI).reshape(1, 128))
        yb = _dsa(proj, qb, qi, w, kit, kt, vb, ksel)
        u, wg, qd, kd, aqk, dec = _gdn_prep(proj, d_conv_w[l], arow.reshape(1, 128),
                                            arow.reshape(128, 1), dtrow.reshape(1, 128),
                                            dtrow.reshape(128, 1))
        yc = _gdn_scan(proj, u, wg, qd, kd, aqk, dec, d_norm_g[l].reshape(1, -1))
        xs = _out_proj(xs, ya, yb, yc, wo[0:1024], wo[1024:3072], wo[3072:4096])
    return xs.reshape(b, s, d)
```

```python
import functools

import jax
import jax.numpy as jnp
import numpy as np
from jax import lax
from jax.experimental import pallas as pl
from jax.experimental.pallas import tpu as pltpu

F32 = jnp.float32
BF16 = jnp.bfloat16
HI = lax.Precision.HIGHEST

HEAD_DIM = 128
H_A = 8
DQK_A = 64
GATE_CAP = 15.0
H_B = 16
HKV_B = 4
Q_LORA = 896
IDX_H = 16
IDX_D = 64
TOPK = 256
QBLK = 128
H_C = 8
CONV_K = 4
EPS = 1e-6

OFF_BCQ = 0
OFF_SMALL = 896
OFF_AQ = 1024
OFF_AK = 1536
OFF_AV = 2048
OFF_AO = 3072
OFF_AZ = 4096
OFF_CQ = 5120
OFF_CK = 6144
OFF_CV = 7168
OFF_CZ = 8192
OFF_BK = 9216
OFF_BV = 9728
OFF_BZ = 10240
N_PROJ = 12288
SM_KI = 0
SM_AI = 64
SM_AF = 72
SM_WI = 80
SM_CB = 96
SM_CA = 104

NEG_BIG = -1e30
LOG2E = 1.4426950408889634
KEY_NEG_INF = -2139095041

VMEM_LIMIT = 56 * 1024 * 1024


def _dot(a, b, prec=None):
    return jnp.dot(a, b, preferred_element_type=F32, precision=prec)


def _dot_nt(a, b, prec=None):
    return lax.dot_general(a, b, (((1,), (1,)), ((), ())), preferred_element_type=F32,
                           precision=prec)


def _dot_tn(a, b, prec=None):
    return lax.dot_general(a, b, (((0,), (0,)), ((), ())), preferred_element_type=F32,
                           precision=prec)


def _iota2(shape, dim):
    return lax.broadcasted_iota(jnp.int32, shape, dim)


def _log_sigmoid(x):
    return jnp.minimum(x, 0.0) - jnp.log(1.0 + jnp.exp(-jnp.abs(x)))


def _softplus(x):
    return jnp.maximum(x, 0.0) + jnp.log(1.0 + jnp.exp(-jnp.abs(x)))


def _soft_cap(x):
    return GATE_CAP * jnp.tanh(x / GATE_CAP)


def _silu(x):
    return x * jax.nn.sigmoid(x)


def _params(sem):
    return pltpu.CompilerParams(dimension_semantics=sem, vmem_limit_bytes=VMEM_LIMIT)


def _rmsnorm_kernel(x_ref, g_ref, o_ref):
    x = x_ref[...]
    ms = jnp.mean(x * x, axis=-1, keepdims=True)
    o_ref[...] = (x * lax.rsqrt(ms + EPS) * g_ref[...]).astype(o_ref.dtype)


def _rmsnorm(x, g, tm=512):
    s, d = x.shape
    return pl.pallas_call(
        _rmsnorm_kernel,
        grid=(s // tm,),
        in_specs=[pl.BlockSpec((tm, d), lambda i: (i, 0)),
                  pl.BlockSpec((1, d), lambda i: (0, 0))],
        out_specs=pl.BlockSpec((tm, d), lambda i: (i, 0)),
        out_shape=jax.ShapeDtypeStruct((s, d), BF16),
        compiler_params=_params(("parallel",)),
        name="rmsnorm",
    )(x, g.reshape(1, d))


def _matmul_kernel(a_ref, b_ref, o_ref):
    o_ref[...] = _dot(a_ref[...], b_ref[...])


def _in_proj(h, w, tm=1024, tn=512):
    s, d = h.shape
    n = w.shape[1]
    tm = min(tm, s)
    return pl.pallas_call(
        _matmul_kernel,
        grid=(s // tm, n // tn),
        in_specs=[pl.BlockSpec((tm, d), lambda i, j: (i, 0)),
                  pl.BlockSpec((d, tn), lambda i, j: (0, j))],
        out_specs=pl.BlockSpec((tm, tn), lambda i, j: (i, j)),
        out_shape=jax.ShapeDtypeStruct((s, n), F32),
        compiler_params=_params(("parallel", "arbitrary")),
        name="in_proj",
    )(h, w)


def _out_proj_kernel(x_ref, ya_ref, yb_ref, yc_ref, wa_ref, wb_ref, wc_ref, o_ref):
    acc = _dot(ya_ref[...], wa_ref[...])
    acc = acc + _dot(yb_ref[...], wb_ref[...])
    acc = acc + _dot(yc_ref[...], wc_ref[...])
    o_ref[...] = x_ref[...] + acc


def _out_proj(x, ya, yb, yc, wa, wb, wc, tm=1024, tn=512):
    s, d = x.shape
    tm = min(tm, s)
    na, nb, nc = ya.shape[1], yb.shape[1], yc.shape[1]
    return pl.pallas_call(
        _out_proj_kernel,
        grid=(s // tm, d // tn),
        in_specs=[pl.BlockSpec((tm, tn), lambda i, j: (i, j)),
                  pl.BlockSpec((tm, na), lambda i, j: (i, 0)),
                  pl.BlockSpec((tm, nb), lambda i, j: (i, 0)),
                  pl.BlockSpec((tm, nc), lambda i, j: (i, 0)),
                  pl.BlockSpec((na, tn), lambda i, j: (0, j)),
                  pl.BlockSpec((nb, tn), lambda i, j: (0, j)),
                  pl.BlockSpec((nc, tn), lambda i, j: (0, j))],
        out_specs=pl.BlockSpec((tm, tn), lambda i, j: (i, j)),
        out_shape=jax.ShapeDtypeStruct((s, d), F32),
        compiler_params=_params(("parallel", "arbitrary")),
        name="out_proj",
    )(x, ya, yb, yc, wa, wb, wc)


def _mlstm_kernel(q_ref, k_ref, v_ref, o_ref, z_ref, sm_ref, bias_ref, ng_ref, y_ref,
                  c_ref, n_ref, m_ref, *, L):
    @pl.when(pl.program_id(0) == 0)
    def _():
        c_ref[...] = jnp.zeros_like(c_ref)
        n_ref[...] = jnp.zeros_like(n_ref)
        m_ref[...] = jnp.zeros_like(m_ref)

    sm = sm_ref[...] + bias_ref[...]
    gcol = _soft_cap(sm)
    lscol = _log_sigmoid(gcol)
    grow = _soft_cap(sm.T[SM_AI:SM_AI + 16, :])
    lsrow = _log_sigmoid(grow)
    ri = _iota2((L, L), 0)
    ci = _iota2((L, L), 1)
    causal = ci <= ri
    tril = jnp.where(causal, 1.0, 0.0).astype(F32)
    triu = jnp.where(ri <= ci, 1.0, 0.0).astype(F32)
    bcol = _dot(tril, lscol, HI)
    brow = _dot(lsrow, triu, HI)
    scale = DQK_A ** -0.5

    for h in range(H_A):
        b_c = bcol[:, SM_AF + h:SM_AF + h + 1]
        li_c = gcol[:, SM_AI + h:SM_AI + h + 1]
        b_r = brow[8 + h:9 + h, :]
        li_r = grow[h:h + 1, :]
        b_last = b_r[:, L - 1:L]
        m_st = m_ref[h:h + 1, 0:1]
        qh = (q_ref[:, h * DQK_A:(h + 1) * DQK_A] * scale).astype(BF16)
        kf = k_ref[:, h * DQK_A:(h + 1) * DQK_A]
        kh = kf.astype(BF16)
        vh = v_ref[:, h * HEAD_DIM:(h + 1) * HEAD_DIM].astype(BF16)
        c_st = c_ref[h]
        n_st = n_ref[h:h + 1, :]

        d_log = jnp.where(causal, b_c - b_r + li_r, -jnp.inf)
        inter = b_c + m_st
        m_j = jnp.maximum(inter, jnp.max(d_log, axis=-1, keepdims=True))
        p = jnp.exp(d_log - m_j) * _dot_nt(qh, kh)
        w_inter = jnp.exp(inter - m_j)
        num = _dot(p.astype(BF16), vh) + w_inter * _dot(qh, c_st.astype(BF16))
        qn = jnp.sum(qh.astype(F32) * n_st, axis=-1, keepdims=True)
        den = jnp.sum(p, axis=-1, keepdims=True) + w_inter * qn
        h_out = num / jnp.maximum(jnp.abs(den), jnp.exp(-m_j))

        u_c = b_last - b_c + li_c
        m_new = jnp.maximum(b_last + m_st, jnp.max(u_c, axis=0, keepdims=True))
        w_u = jnp.exp(u_c - m_new)
        decay = jnp.exp(b_last + m_st - m_new)
        kw = kf * w_u
        c_ref[h] = decay * c_st + _dot_tn(kw.astype(BF16), vh)
        n_ref[h:h + 1, :] = decay * n_st + jnp.sum(kw, axis=0, keepdims=True)
        m_ref[h:h + 1, :] = jnp.broadcast_to(m_new, (1, 128))

        g = ng_ref[:, h * HEAD_DIM:(h + 1) * HEAD_DIM]
        hn = h_out * lax.rsqrt(jnp.mean(h_out * h_out, axis=-1, keepdims=True) + EPS) * g
        oh = o_ref[:, h * HEAD_DIM:(h + 1) * HEAD_DIM]
        zh = z_ref[:, h * HEAD_DIM:(h + 1) * HEAD_DIM]
        y_ref[:, h * HEAD_DIM:(h + 1) * HEAD_DIM] = (jax.nn.sigmoid(oh) * hn * _silu(zh)).astype(BF16)


def _mlstm(proj, bias_row, norm_g, L=256):
    s = proj.shape[0]
    L = min(L, s)
    return pl.pallas_call(
        functools.partial(_mlstm_kernel, L=L),
        grid=(s // L,),
        in_specs=[pl.BlockSpec((L, 512), lambda i: (i, OFF_AQ // 512)),
                  pl.BlockSpec((L, 512), lambda i: (i, OFF_AK // 512)),
                  pl.BlockSpec((L, 1024), lambda i: (i, OFF_AV // 1024)),
                  pl.BlockSpec((L, 1024), lambda i: (i, OFF_AO // 1024)),
                  pl.BlockSpec((L, 1024), lambda i: (i, OFF_AZ // 1024)),
                  pl.BlockSpec((L, 128), lambda i: (i, OFF_SMALL // 128)),
                  pl.BlockSpec((1, 128), lambda i: (0, 0)),
                  pl.BlockSpec((1, 1024), lambda i: (0, 0))],
        out_specs=pl.BlockSpec((L, 1024), lambda i: (i, 0)),
        out_shape=jax.ShapeDtypeStruct((s, 1024), BF16),
        scratch_shapes=[pltpu.VMEM((H_A, DQK_A, HEAD_DIM), F32),
                        pltpu.VMEM((H_A, DQK_A), F32),
                        pltpu.VMEM((H_A, 128), F32)],
        compiler_params=_params(("arbitrary",)),
        name="mlstm",
    )(proj, proj, proj, proj, proj, proj, bias_row, norm_g)


GC = 64


def _split_bf16(a):
    hi = a.astype(BF16)
    lo = (a - hi.astype(F32)).astype(BF16)
    return hi, lo


def _dot3(a, b):
    ah, al = _split_bf16(a)
    bh, bl = _split_bf16(b)
    return _dot(ah, bh) + (_dot(ah, bl) + _dot(al, bh))


def _gdn_prep_kernel(q_ref, k_ref, v_ref, hq_ref, hk_ref, hv_ref, sm_ref, cw_ref,
                     arow_ref, acol_ref, dtrow_ref, dtcol_ref,
                     u_ref, w_ref, qd_ref, kd_ref, aqk_ref, dec_ref):
    i = pl.program_id(0)

    def conv_silu(x_ref, h_ref, w):
        halo = jnp.where(i > 0, h_ref[...], 0.0)
        xf = jnp.concatenate([halo, x_ref[...]], axis=0)
        y = w[3:4, :] * xf[8:8 + GC]
        for sh in range(1, CONV_K):
            y = y + w[3 - sh:4 - sh, :] * pltpu.roll(xf, sh, 0)[8:8 + GC]
        return _silu(y)

    qc = conv_silu(q_ref, hq_ref, cw_ref[:, 0:1024])
    kc = conv_silu(k_ref, hk_ref, cw_ref[:, 1024:2048])
    vc = conv_silu(v_ref, hv_ref, cw_ref[:, 2048:3072])

    sm = sm_ref[...]
    smT = sm.T
    beta_col = jax.nn.sigmoid(sm)
    g_col = -jnp.exp(arow_ref[...]) * _softplus(sm + dtrow_ref[...])
    g_row = -jnp.exp(acol_ref[SM_CA:SM_CA + 8, :]) * _softplus(
        smT[SM_CA:SM_CA + 8, :] + dtcol_ref[SM_CA:SM_CA + 8, :])
    ri = _iota2((GC, GC), 0)
    ci = _iota2((GC, GC), 1)
    causal = ci <= ri
    strict = ci < ri
    bdiag = (ri // 16) == (ci // 16)
    eye = jnp.where(ri == ci, 1.0, 0.0).astype(F32)
    tril = jnp.where(causal, 1.0, 0.0).astype(F32)
    triu = jnp.where(ri <= ci, 1.0, 0.0).astype(F32)
    gc_col = _dot(tril, g_col, HI)
    gc_row = _dot(g_row, triu, HI)
    scale = HEAD_DIM ** -0.5

    heads = range(H_C)
    sls = [slice(h * HEAD_DIM, (h + 1) * HEAD_DIM) for h in heads]

    def each(fn, *lists):
        return [fn(*args) for args in zip(*lists)]

    qh = [qc[:, sl] for sl in sls]
    kh = [kc[:, sl] for sl in sls]
    vh = [vc[:, sl] for sl in sls]
    qh = each(lambda t: t * lax.rsqrt(jnp.sum(t * t, axis=-1, keepdims=True) + EPS) * scale, qh)
    kh = each(lambda t: t * lax.rsqrt(jnp.sum(t * t, axis=-1, keepdims=True) + EPS), kh)
    gc_c = [gc_col[:, SM_CA + h:SM_CA + h + 1] for h in heads]
    gc_r = [gc_row[h:h + 1, :] for h in heads]
    beta_c = [beta_col[:, SM_CB + h:SM_CB + h + 1] for h in heads]
    g_last = [t[GC - 1:GC, :] for t in gc_c]
    e_gc = each(jnp.exp, gc_c)
    gamma = each(lambda c, r: jnp.where(causal, jnp.exp(jnp.where(causal, c - r, 0.0)), 0.0),
                 gc_c, gc_r)
    kb = each(lambda k, b: k * b, kh, beta_c)
    kh_b = each(lambda k: k.astype(BF16), kh)
    a = each(lambda kbh, kbf, gm: jnp.where(strict, _dot_nt(kbh.astype(BF16), kbf) * gm, 0.0),
             kb, kh_b, gamma)
    ad = each(lambda t: jnp.where(bdiag, t, 0.0), a)
    ao = each(lambda t, d: t - d, a, ad)
    a2 = each(_dot3, ad, ad)
    a4 = each(_dot3, a2, a2)
    a8 = each(_dot3, a4, a4)
    dinv = each(lambda d: eye - d, ad)
    for pw in (a2, a4, a8):
        dinv = each(lambda d, p: d + _dot3(d, p), dinv, pw)
    nn = each(_dot3, dinv, ao)
    n2 = each(_dot3, nn, nn)
    ninv = each(lambda n: eye - n, nn)
    ninv = each(lambda n, p: n + _dot3(n, p), ninv, n2)
    tinv = each(lambda n, d: _dot3(n, d).astype(BF16), ninv, dinv)
    u = each(lambda t, v, b: _dot(t, (v * b).astype(BF16)), tinv, vh, beta_c)
    w = each(lambda t, k, e: _dot(t, (k * e).astype(BF16)), tinv, kb, e_gc)
    aqk = each(lambda q, kbf, gm: _dot_nt(q.astype(BF16), kbf) * gm, qh, kh_b, gamma)
    for h in heads:
        sl = sls[h]
        u_ref[:, sl] = u[h]
        w_ref[:, sl] = w[h].astype(BF16)
        qd_ref[:, sl] = (qh[h] * e_gc[h]).astype(BF16)
        kd_ref[:, sl] = (kh[h] * jnp.exp(g_last[h] - gc_c[h])).astype(BF16)
        aqk_ref[h] = aqk[h].astype(BF16)
        dec_ref[0, h:h + 1, :] = jnp.broadcast_to(jnp.exp(g_last[h]), (1, 128))


def _gdn_prep(proj, conv_w, arow, acol, dtrow, dtcol):
    s = proj.shape[0]
    nch = s // GC

    def halo(col):
        return pl.BlockSpec((8, 1024), lambda i: (jnp.maximum(i * (GC // 8) - 1, 0), col))

    row = lambda i: (i, 0)
    return pl.pallas_call(
        _gdn_prep_kernel,
        grid=(nch,),
        in_specs=[pl.BlockSpec((GC, 1024), lambda i: (i, OFF_CQ // 1024)),
                  pl.BlockSpec((GC, 1024), lambda i: (i, OFF_CK // 1024)),
                  pl.BlockSpec((GC, 1024), lambda i: (i, OFF_CV // 1024)),
                  halo(OFF_CQ // 1024), halo(OFF_CK // 1024), halo(OFF_CV // 1024),
                  pl.BlockSpec((GC, 128), lambda i: (i, OFF_SMALL // 128)),
                  pl.BlockSpec((CONV_K, 3072), lambda i: (0, 0)),
                  pl.BlockSpec((1, 128), lambda i: (0, 0)),
                  pl.BlockSpec((128, 1), lambda i: (0, 0)),
                  pl.BlockSpec((1, 128), lambda i: (0, 0)),
                  pl.BlockSpec((128, 1), lambda i: (0, 0))],
        out_specs=[pl.BlockSpec((GC, 1024), row), pl.BlockSpec((GC, 1024), row),
                   pl.BlockSpec((GC, 1024), row), pl.BlockSpec((GC, 1024), row),
                   pl.BlockSpec((H_C, GC, GC), lambda i: (0, i, 0)),
                   pl.BlockSpec((1, H_C, 128), lambda i: (i, 0, 0))],
        out_shape=[jax.ShapeDtypeStruct((s, 1024), F32),
                   jax.ShapeDtypeStruct((s, 1024), BF16),
                   jax.ShapeDtypeStruct((s, 1024), BF16),
                   jax.ShapeDtypeStruct((s, 1024), BF16),
                   jax.ShapeDtypeStruct((H_C, s, GC), BF16),
                   jax.ShapeDtypeStruct((nch, H_C, 128), F32)],
        compiler_params=_params(("parallel",)),
        name="gdn_prep",
    )(proj, proj, proj, proj, proj, proj, proj, conv_w, arow, acol, dtrow, dtcol)


def _gdn_scan_kernel(u_ref, w_ref, qd_ref, kd_ref, aqk_ref, dec_ref, z_ref, ng_ref, y_ref,
                     s_ref, *, NCH):
    @pl.when(pl.program_id(0) == 0)
    def _():
        s_ref[...] = jnp.zeros_like(s_ref)

    heads = range(H_C)
    sls = [slice(h * HEAD_DIM, (h + 1) * HEAD_DIM) for h in heads]
    for c in range(NCH):
        rs = slice(c * GC, (c + 1) * GC)
        st = [s_ref[h] for h in heads]
        st_b = [t.astype(BF16) for t in st]
        v_new = [(u_ref[rs, sls[h]] - _dot(w_ref[rs, sls[h]], st_b[h])).astype(BF16)
                 for h in heads]
        for h in heads:
            s_ref[h] = dec_ref[c, h:h + 1, :] * st[h] + _dot_tn(kd_ref[rs, sls[h]], v_new[h])
        o = [_dot(qd_ref[rs, sls[h]], st_b[h]) + _dot(aqk_ref[h, rs, :], v_new[h])
             for h in heads]
        for h in heads:
            on = o[h] * lax.rsqrt(jnp.mean(o[h] * o[h], axis=-1, keepdims=True) + EPS) * ng_ref[...]
            y_ref[rs, sls[h]] = (on * _silu(z_ref[rs, sls[h]])).astype(BF16)


def _gdn_scan(proj, u, w, qd, kd, aqk, dec, norm_g, nch=4):
    s = proj.shape[0]
    tt = nch * GC
    row = lambda i: (i, 0)
    return pl.pallas_call(
        functools.partial(_gdn_scan_kernel, NCH=nch),
        grid=(s // tt,),
        in_specs=[pl.BlockSpec((tt, 1024), row), pl.BlockSpec((tt, 1024), row),
                  pl.BlockSpec((tt, 1024), row), pl.BlockSpec((tt, 1024), row),
                  pl.BlockSpec((H_C, tt, GC), lambda i: (0, i, 0)),
                  pl.BlockSpec((nch, H_C, 128), lambda i: (i, 0, 0)),
                  pl.BlockSpec((tt, 1024), lambda i: (i, OFF_CZ // 1024)),
                  pl.BlockSpec((1, 128), lambda i: (0, 0))],
        out_specs=pl.BlockSpec((tt, 1024), row),
        out_shape=jax.ShapeDtypeStruct((s, 1024), BF16),
        scratch_shapes=[pltpu.VMEM((H_C, HEAD_DIM, HEAD_DIM), F32)],
        compiler_params=_params(("arbitrary",)),
        name="gdn_scan",
    )(u, w, qd, kd, aqk, dec, proj, norm_g)


def _dsa_prep_kernel(cq_ref, k_ref, v_ref, sm_ref, cqg_ref, wuq_ref, wuqi_ref, qng_ref,
                     kng_ref, kig_ref, kib_ref,
                     qb_ref, qi_ref, w_ref, kt_ref, vb_ref, kit_ref, kn2_ref):
    cq = cq_ref[...]
    cq = cq * lax.rsqrt(jnp.mean(cq * cq, axis=-1, keepdims=True) + EPS) * cqg_ref[...]
    cq = cq.astype(BF16)
    qf = _dot(cq, wuq_ref[...])
    qscale = HEAD_DIM ** -0.5 * LOG2E
    for h in range(H_B):
        qh = qf[:, h * HEAD_DIM:(h + 1) * HEAD_DIM]
        qh = qh * lax.rsqrt(jnp.mean(qh * qh, axis=-1, keepdims=True) + EPS) * qng_ref[...]
        qb_ref[h] = (qh * qscale).astype(BF16)
        qi_ref[h] = _dot(cq, wuqi_ref[h]).astype(BF16)
    kn = []
    for g in range(HKV_B):
        kg = k_ref[:, g * HEAD_DIM:(g + 1) * HEAD_DIM]
        kn.append(kg * lax.rsqrt(jnp.mean(kg * kg, axis=-1, keepdims=True) + EPS) * kng_ref[...])
    kt_b = jnp.concatenate(kn, axis=1).T.astype(BF16)
    kt_ref[...] = kt_b
    kt_f = kt_b.astype(F32)
    kn2 = [jnp.sum(jnp.square(kt_f[g * HEAD_DIM:(g + 1) * HEAD_DIM, :]), axis=0, keepdims=True)
           for g in range(HKV_B)]
    kn2_ref[...] = jnp.concatenate(kn2 + [jnp.zeros_like(kn2[0])] * (8 - HKV_B), axis=0)
    vb_ref[...] = v_ref[...].astype(BF16)
    sm = sm_ref[...]
    w_ref[...] = sm * (IDX_H ** -0.5 * IDX_D ** -0.5)
    lane = _iota2(sm.shape, 1)
    is_ki = lane < IDX_D
    mu = jnp.sum(jnp.where(is_ki, sm, 0.0), axis=-1, keepdims=True) * (1.0 / IDX_D)
    xc = jnp.where(is_ki, sm - mu, 0.0)
    var = jnp.sum(xc * xc, axis=-1, keepdims=True) * (1.0 / IDX_D)
    ki = xc * lax.rsqrt(var + EPS) * kig_ref[...] + kib_ref[...]
    kit_ref[...] = ki.T[0:IDX_D, :].astype(BF16)


def _dsa_prep(proj, cq_g, wuq, wuqi_r, qn_g, kn_g, ki_g, ki_b, tt=256):
    s = proj.shape[0]
    tt = min(tt, s)
    const2 = lambda i: (0, 0)
    return pl.pallas_call(
        _dsa_prep_kernel,
        grid=(s // tt,),
        in_specs=[pl.BlockSpec((tt, Q_LORA), lambda i: (i, 0)),
                  pl.BlockSpec((tt, 512), lambda i: (i, OFF_BK // 512)),
                  pl.BlockSpec((tt, 512), lambda i: (i, OFF_BV // 512)),
                  pl.BlockSpec((tt, 128), lambda i: (i, OFF_SMALL // 128)),
                  pl.BlockSpec((1, Q_LORA), const2),
                  pl.BlockSpec((Q_LORA, H_B * HEAD_DIM), const2),
                  pl.BlockSpec((IDX_H, Q_LORA, IDX_D), lambda i: (0, 0, 0)),
                  pl.BlockSpec((1, HEAD_DIM), const2),
                  pl.BlockSpec((1, HEAD_DIM), const2),
                  pl.BlockSpec((1, 128), const2),
                  pl.BlockSpec((1, 128), const2)],
        out_specs=[pl.BlockSpec((H_B, tt, HEAD_DIM), lambda i: (0, i, 0)),
                   pl.BlockSpec((IDX_H, tt, IDX_D), lambda i: (0, i, 0)),
                   pl.BlockSpec((tt, 128), lambda i: (i, 0)),
                   pl.BlockSpec((HKV_B * HEAD_DIM, tt), lambda i: (0, i)),
                   pl.BlockSpec((tt, HKV_B * HEAD_DIM), lambda i: (i, 0)),
                   pl.BlockSpec((IDX_D, tt), lambda i: (0, i)),
                   pl.BlockSpec((8, tt), lambda i: (0, i))],
        out_shape=[jax.ShapeDtypeStruct((H_B, s, HEAD_DIM), BF16),
                   jax.ShapeDtypeStruct((IDX_H, s, IDX_D), BF16),
                   jax.ShapeDtypeStruct((s, 128), F32),
                   jax.ShapeDtypeStruct((HKV_B * HEAD_DIM, s), BF16),
                   jax.ShapeDtypeStruct((s, HKV_B * HEAD_DIM), BF16),
                   jax.ShapeDtypeStruct((IDX_D, s), BF16),
                   jax.ShapeDtypeStruct((8, s), F32)],
        compiler_params=_params(("parallel",)),
        name="dsa_prep",
    )(proj, proj, proj, proj, cq_g, wuq, wuqi_r, qn_g, kn_g, ki_g, ki_b)


TKS = 512


def _key_of(f):
    bits = lax.bitcast_convert_type(f, jnp.int32)
    return bits ^ ((bits >> 31) & jnp.int32(0x7FFFFFFF))


def _float_of(k):
    return lax.bitcast_convert_type(k ^ ((k >> 31) & jnp.int32(0x7FFFFFFF)), F32)


SHIFT_LIMIT = 40.0


def _dsa_kernel(qb_ref, qi_ref, w_ref, kit_ref, kt_ref, vb_ref, kn2_ref, z_ref, y_ref,
                keys_ref, bias_ref, wb_ref, qaug_ref, kaug_ref, vcat_ref, sbuf_ref, pcat_ref,
                mb_ref, alpha_ref, m_ref, acc_ref, *, TK, KSEL):
    i = pl.program_id(0)
    g = pl.program_id(1)
    n_t = ((i + 1) * QBLK + TK - 1) // TK
    NS = TK // TKS
    assert NS == 2
    GH = H_B // HKV_B

    def key_chunk(off, c):
        return keys_ref[:, pl.ds(pl.multiple_of(off + c * 128, 128), 128)]

    @pl.when(g == 0)
    def _():
        wt = w_ref[...]
        for h in range(IDX_H):
            wb_ref[h] = jnp.broadcast_to(wt[:, SM_WI + h:SM_WI + h + 1], (QBLK, 128))
        qi2 = qi_ref[...].reshape(IDX_H * QBLK, IDX_D)
        row = i * QBLK + _iota2((QBLK, 128), 0)
        lane = _iota2((QBLK, 128), 1)

        def score_tile(j, carry):
            mx, mn = carry
            for sub in range(NS):
                off = pl.multiple_of(j * TK + sub * TKS, TKS)
                lg = _dot(qi2, kit_ref[:, pl.ds(off, TKS)])
                for c in range(TKS // 128):
                    s = jnp.zeros((QBLK, 128), F32)
                    for h in range(IDX_H):
                        s = s + wb_ref[h] * jnp.maximum(
                            lg[h * QBLK:(h + 1) * QBLK, c * 128:(c + 1) * 128], 0.0)
                    valid = off + c * 128 + lane <= row
                    mx = jnp.maximum(mx, jnp.where(valid, s, -jnp.inf))
                    mn = jnp.minimum(mn, jnp.where(valid, s, jnp.inf))
                    keys_ref[:, pl.ds(pl.multiple_of(off + c * 128, 128), 128)] = _key_of(
                        jnp.where(valid, s, -jnp.inf))
            return mx, mn

        mx, mn = lax.fori_loop(0, n_t, score_tile,
                               (jnp.full((QBLK, 128), -jnp.inf, F32),
                                jnp.full((QBLK, 128), jnp.inf, F32)))
        mx = jnp.max(mx, axis=1, keepdims=True)
        mn = jnp.min(mn, axis=1, keepdims=True)

        def count_ge(cand):
            cand_b = jnp.broadcast_to(cand, (QBLK, 128))

            def count_tile(j, acc):
                off = pl.multiple_of(j * TK, TK)
                for c in range(TK // 128):
                    acc = acc + jnp.where(key_chunk(off, c) >= cand_b, 1, 0)
                return acc

            acc = lax.fori_loop(0, n_t, count_tile, jnp.zeros((QBLK, 128), jnp.int32))
            return jnp.sum(acc, axis=1, keepdims=True).astype(F32)

        nvalid = i * QBLK + _iota2((QBLK, 1), 0) + 1
        done0 = nvalid <= KSEL
        log_k = float(np.log(KSEL))

        def cond(st):
            return jnp.logical_and(st[0] < 160, st[1] > 0)

        def body(st):
            it, _, lo, hi, clo, chi, thr, done_i, side = st
            done = done_i != 0
            adj = jnp.logical_and(hi == lo + 1, jnp.logical_not(done))
            thr = jnp.where(adj, lo, thr)
            done = jnp.logical_or(done, adj)
            lo_f = _float_of(lo)
            hi_f = _float_of(hi)
            d_lo = jnp.log(clo) - log_k
            d_hi = log_k - jnp.log(jnp.maximum(chi, 0.5))
            d_lo = jnp.where(side <= -2, 0.5 * d_lo, d_lo)
            d_hi = jnp.where(side >= 2, 0.5 * d_hi, d_hi)
            cand_i = _key_of(lo_f + (hi_f - lo_f) * (d_lo / (d_lo + d_hi)))
            mid = (lo >> 1) + (hi >> 1) + (lo & hi & 1)
            cand = jnp.where(it % 4 == 3, mid, cand_i)
            cand = jnp.minimum(jnp.maximum(cand, lo + 1), hi - 1)
            cand = jnp.where(done, thr, cand)
            cnt_f = count_ge(cand)
            ge = cnt_f >= KSEL
            eq = jnp.logical_and(cnt_f == KSEL, jnp.logical_not(done))
            thr = jnp.where(eq, cand, thr)
            done = jnp.logical_or(done, eq)
            lo = jnp.where(ge, cand, lo)
            clo = jnp.where(ge, cnt_f, clo)
            hi = jnp.where(ge, hi, cand)
            chi = jnp.where(ge, chi, cnt_f)
            side = jnp.where(ge, jnp.where(side > 0, side + 1, 1), jnp.where(side < 0, side - 1, -1))
            n_open = jnp.sum(jnp.where(done, 0, 1))
            return it + 1, n_open, lo, hi, clo, chi, thr, jnp.where(done, 1, 0), side

        st0 = (jnp.int32(0), jnp.sum(jnp.where(done0, 0, 1)),
               _key_of(mn), _key_of(mx) + 1, nvalid.astype(F32), jnp.zeros((QBLK, 1), F32),
               jnp.full((QBLK, 1), KEY_NEG_INF + 1, jnp.int32), jnp.where(done0, 1, 0),
               jnp.zeros((QBLK, 1), jnp.int32))
        st = lax.while_loop(cond, body, st0)
        thr = jnp.maximum(jnp.where(st[7] != 0, st[6], st[2]), KEY_NEG_INF + 1)
        thr_b = jnp.broadcast_to(thr, (QBLK, 128))

        def bias_tile(j, carry):
            off = pl.multiple_of(j * TK, TK)
            for c in range(TK // 128):
                bias_ref[:, pl.ds(pl.multiple_of(off + c * 128, 128), 128)] = jnp.where(
                    key_chunk(off, c) >= thr_b, 0.0, NEG_BIG).astype(BF16)
            return carry

        lax.fori_loop(0, n_t, bias_tile, 0)

    acc_ref[...] = jnp.zeros_like(acc_ref)
    qg = qb_ref[...].reshape(GH * QBLK, HEAD_DIM)
    qaug_ref[:, 0:HEAD_DIM] = qg
    ri = _iota2((GH * QBLK, QBLK), 0)
    ci = _iota2((GH * QBLK, QBLK), 1)
    qaug_ref[:, HEAD_DIM:2 * HEAD_DIM] = jnp.where((ri % QBLK) == ci, 1.0, 0.0).astype(BF16)
    vcat_ref[:, HEAD_DIM:2 * HEAD_DIM] = jnp.ones((TK, HEAD_DIM), BF16)

    qf = qg.astype(F32)
    qnorm = jnp.sqrt(jnp.sum(qf * qf, axis=-1, keepdims=True))
    kmax = jnp.sqrt(jnp.max(kn2_ref[pl.ds(g, 1), :], axis=-1, keepdims=True))
    bound = 1.01 * (qnorm * kmax) + 1e-3
    mb_ref[...] = jnp.broadcast_to(bound, (GH * QBLK, 128))
    static_shift = jnp.max(bound) < SHIFT_LIMIT

    def scores(t, slot):
        off = pl.multiple_of(t * TKS, TKS)
        kaug_ref[slot, 0:HEAD_DIM, :] = kt_ref[:, pl.ds(off, TKS)]
        kaug_ref[slot, HEAD_DIM:2 * HEAD_DIM, :] = bias_ref[:, pl.ds(off, TKS)]
        sbuf_ref[slot] = _dot(qaug_ref[...], kaug_ref[slot])

    def load_values(t, slot):
        off = pl.multiple_of(t * TKS, TKS)
        vcat_ref[slot * TKS:(slot + 1) * TKS, 0:HEAD_DIM] = vb_ref[pl.ds(off, TKS), :]

    def run_pipeline(consume, finish_pair):
        def attn_tile(j, carry):
            t0 = j * NS
            scores(t0 + 1, 1)
            consume(t0, 0)
            scores(t0 + 2, 0)
            consume(t0 + 1, 1)
            finish_pair()
            return carry

        scores(0, 0)
        lax.fori_loop(0, n_t - 1, attn_tile, 0)
        t_last = (n_t - 1) * NS
        scores(t_last + 1, 1)
        consume(t_last, 0)
        consume(t_last + 1, 1)
        finish_pair()

    @pl.when(static_shift)
    def _():
        def consume(t, slot):
            for hh in range(GH):
                rs = slice(hh * QBLK, (hh + 1) * QBLK)
                for c in range(TKS // 128):
                    cs = slice(c * 128, (c + 1) * 128)
                    pcat_ref[rs, slot * TKS + c * 128:slot * TKS + (c + 1) * 128] = jnp.exp2(
                        sbuf_ref[slot, rs, cs] - mb_ref[rs, :]).astype(BF16)
            load_values(t, slot)

        def finish_pair():
            acc_ref[...] = acc_ref[...] + _dot(pcat_ref[...], vcat_ref[...])

        run_pipeline(consume, finish_pair)

    @pl.when(jnp.logical_not(static_shift))
    def _():
        m_ref[...] = jnp.full_like(m_ref, NEG_BIG)

        def consume(t, slot):
            cs = slice(slot * TKS, (slot + 1) * TKS)
            for hh in range(GH):
                rs = slice(hh * QBLK, (hh + 1) * QBLK)
                sh = sbuf_ref[slot, rs, :]
                m_old = m_ref[rs]
                m_new = jnp.maximum(m_old, jnp.max(sh, axis=-1, keepdims=True))
                pcat_ref[rs, cs] = jnp.exp2(sh - m_new).astype(BF16)
                alpha_ref[rs, :] = jnp.exp2(m_old - m_new)
                m_ref[rs] = m_new
            load_values(t, slot)
            pv = _dot(pcat_ref[:, cs], vcat_ref[cs, :])
            acc_ref[...] = alpha_ref[...] * acc_ref[...] + pv

        run_pipeline(consume, lambda: None)

    for hh in range(GH):
        rs = slice(hh * QBLK, (hh + 1) * QBLK)
        out = acc_ref[rs, 0:HEAD_DIM] / acc_ref[rs, HEAD_DIM:2 * HEAD_DIM]
        sl = slice(hh * HEAD_DIM, (hh + 1) * HEAD_DIM)
        y_ref[:, sl] = (out * _silu(z_ref[:, sl])).astype(BF16)


def _dsa(proj, qb, qi, w, kit, kt, vb, kn2, ksel, tk=1024):
    s = proj.shape[0]
    tk = min(tk, s)
    ns = tk // TKS
    gh = H_B // HKV_B
    return pl.pallas_call(
        functools.partial(_dsa_kernel, TK=tk, KSEL=ksel),
        grid=(s // QBLK, HKV_B),
        in_specs=[pl.BlockSpec((gh, QBLK, HEAD_DIM), lambda i, g: (g, i, 0)),
                  pl.BlockSpec((IDX_H, QBLK, IDX_D), lambda i, g: (0, i, 0)),
                  pl.BlockSpec((QBLK, 128), lambda i, g: (i, 0)),
                  pl.BlockSpec((IDX_D, s), lambda i, g: (0, 0)),
                  pl.BlockSpec((HEAD_DIM, s), lambda i, g: (g, 0)),
                  pl.BlockSpec((s, HEAD_DIM), lambda i, g: (0, g)),
                  pl.BlockSpec((8, s), lambda i, g: (0, 0)),
                  pl.BlockSpec((QBLK, gh * HEAD_DIM), lambda i, g: (i, OFF_BZ // (gh * HEAD_DIM) + g))],
        out_specs=pl.BlockSpec((QBLK, gh * HEAD_DIM), lambda i, g: (i, g)),
        out_shape=jax.ShapeDtypeStruct((s, H_B * HEAD_DIM), BF16),
        scratch_shapes=[pltpu.VMEM((QBLK, s), jnp.int32),
                        pltpu.VMEM((QBLK, s), BF16),
                        pltpu.VMEM((IDX_H, QBLK, 128), F32),
                        pltpu.VMEM((gh * QBLK, 2 * HEAD_DIM), BF16),
                        pltpu.VMEM((ns, 2 * HEAD_DIM, TKS), BF16),
                        pltpu.VMEM((tk, 2 * HEAD_DIM), BF16),
                        pltpu.VMEM((ns, gh * QBLK, TKS), F32),
                        pltpu.VMEM((gh * QBLK, tk), BF16),
                        pltpu.VMEM((gh * QBLK, 128), F32),
                        pltpu.VMEM((gh * QBLK, 1), F32),
                        pltpu.VMEM((gh * QBLK, 1), F32),
                        pltpu.VMEM((gh * QBLK, 2 * HEAD_DIM), F32)],
        compiler_params=_params(("arbitrary", "arbitrary")),
        name="dsa",
    )(qb, qi, w, kit, kt, vb, kn2, proj)


def _permute_w_in(w):
    sp = np.cumsum([0, 512, 512, 1024, 8, 8, 1024, 1024,
                    896, 512, 512, 64, 16, 2048,
                    3072, 8, 8, 1024])
    names = ["aq", "ak", "av", "ai", "af", "ao", "az", "bcq", "bk", "bv", "bki", "bwi", "bz",
             "cqkv", "cb", "ca", "cz"]
    c = {n: w[:, int(sp[t]):int(sp[t + 1])] for t, n in enumerate(names)}
    pad = jnp.zeros((w.shape[0], 16), w.dtype)
    cols = [c["bcq"], c["bki"], c["ai"], c["af"], c["bwi"], c["cb"], c["ca"], pad,
            c["aq"], c["ak"], c["av"], c["ao"], c["az"], c["cqkv"], c["cz"],
            c["bk"], c["bv"], c["bz"]]
    return jnp.concatenate(cols, axis=1).astype(BF16)


def _small_row(vals, off):
    row = jnp.zeros((128,), F32)
    return row.at[off:off + vals.shape[0]].set(vals.astype(F32))


def kernel(x, norm_g, w_in, w_out, m_ib, m_fb, m_norm_g, a_cq_g, a_wuq, a_wuqi, a_qn_g, a_kn_g,
           a_ki_g, a_ki_b, d_conv_w, d_a_log, d_dt_bias, d_norm_g):
    b, s, d = x.shape
    assert b == 1
    depth = w_in.shape[0]
    ksel = min(TOPK, s // 4)
    xs = x.reshape(s, d)
    for l in range(depth):
        w_r = _permute_w_in(w_in[l])
        wo = w_out[l].astype(BF16)
        gate_bias = (_small_row(m_ib[l], SM_AI) + _small_row(m_fb[l], SM_AF)).reshape(1, 128)
        arow = _small_row(d_a_log[l], SM_CA)
        dtrow = _small_row(d_dt_bias[l], SM_CA)
        wuqi_r = a_wuqi[l].reshape(Q_LORA, IDX_H, IDX_D).transpose(1, 0, 2).astype(BF16)

        h = _rmsnorm(xs, norm_g[l])
        proj = _in_proj(h, w_r)
        ya = _mlstm(proj, gate_bias, m_norm_g[l].reshape(1, -1))
        qb, qi, w, kt, vb, kit, kn2 = _dsa_prep(
            proj, a_cq_g[l].reshape(1, -1), a_wuq[l].astype(BF16), wuqi_r,
            a_qn_g[l].reshape(1, -1), a_kn_g[l].reshape(1, -1),
            _small_row(a_ki_g[l], SM_KI).reshape(1, 128), _small_row(a_ki_b[l], SM_KI).reshape(1, 128))
        yb = _dsa(proj, qb, qi, w, kit, kt, vb, kn2, ksel)
        u, wg, qd, kd, aqk, dec = _gdn_prep(proj, d_conv_w[l], arow.reshape(1, 128),
                                            arow.reshape(128, 1), dtrow.reshape(1, 128),
                                            dtrow.reshape(128, 1))
        yc = _gdn_scan(proj, u, wg, qd, kd, aqk, dec, d_norm_g[l].reshape(1, -1))
        xs = _out_proj(xs, ya, yb, yc, wo[0:1024], wo[1024:3072], wo[3072:4096])
    return xs.reshape(b, s, d)
```
